```python
import math
import jax, jax.numpy as jnp
from jax import lax
import numpy as np

D_MODEL = 1024
BATCH = 32
SEQ = 256
DEPTH = 4
DEC_BATCH = 2
DEC_SEQ = 4096
PAST_LEN = 256

GRID_W = 64
QK_DIM = 64
V_DIM = 2 * QK_DIM
N_HEADS = D_MODEL // (2 * QK_DIM)
FOURIER_GROUPS = 4
FOURIER_WIDTH = D_MODEL // 2
FOURIER_GROUP_DIM = FOURIER_WIDTH // FOURIER_GROUPS
QK_WIDTH = N_HEADS * 2 * QK_DIM
V_WIDTH = N_HEADS * V_DIM
GATE_WIDTH = 2 * D_MODEL
IN_WIDTH = FOURIER_WIDTH + 2 * QK_WIDTH + V_WIDTH + GATE_WIDTH
D_FF = ((8 * D_MODEL // 3 + 127) // 128) * 128
CONV_W = 3
Q_BLOCK = 128
ROPE_BASE = 10000.0
EPS = 1e-6
N_MOD = 6

kernel_name = 'diff_fnet_prefix_dit_step'


def rmsnorm(x, g):
    xf = x.astype(jnp.float32)
    y = xf * lax.rsqrt(jnp.mean(xf * xf, axis=-1, keepdims=True) + EPS)
    return (y * g.astype(jnp.float32)).astype(x.dtype)


def adaln(cond, w, b):
    m = jax.nn.silu(cond) @ w + b
    return [m[:, None, i * D_MODEL:(i + 1) * D_MODEL] for i in range(N_MOD)]


def axial_rope_angles(n_tokens):
    rows = n_tokens // GRID_W
    row = jnp.repeat(jnp.arange(rows), GRID_W).astype(jnp.float32)
    col = jnp.tile(jnp.arange(GRID_W), rows).astype(jnp.float32)
    half = QK_DIM // 2
    inv_freq = 1.0 / (ROPE_BASE ** (jnp.arange(0, half, 2, dtype=jnp.float32) / half))
    return row[:, None] * inv_freq, col[:, None] * inv_freq


def rope_rotate(x, ang):
    n2 = x.shape[-1] // 2
    cos = jnp.cos(ang)[None, :, None, None, :].astype(x.dtype)
    sin = jnp.sin(ang)[None, :, None, None, :].astype(x.dtype)
    x1, x2 = x[..., :n2], x[..., n2:]
    return jnp.concatenate([x1 * cos - x2 * sin, x1 * sin + x2 * cos], axis=-1)


def apply_axial_rope(x, angs):
    row_ang, col_ang = angs
    half = QK_DIM // 2
    return jnp.concatenate([rope_rotate(x[..., :half], row_ang),
                            rope_rotate(x[..., half:], col_ang)], axis=-1)


def diff_attention(q, k, v, lam):
    b, t = q.shape[0], q.shape[1]
    nb = t // Q_BLOCK
    scale = QK_DIM ** -0.5
    qb = jnp.moveaxis(q.reshape(b, nb, Q_BLOCK, N_HEADS, 2, QK_DIM), 1, 0)

    def one_block(qblk):
        s = jnp.einsum('bqhmd,bkhmd->bhmqk', qblk, k).astype(jnp.float32) * scale
        p = jax.nn.softmax(s, axis=-1)
        pd = p[:, :, 0] - lam * p[:, :, 1]
        return jnp.einsum('bhqk,bkhd->bqhd', pd.astype(v.dtype), v)

    o = lax.map(one_block, qb)
    return jnp.moveaxis(o, 0, 1).reshape(b, t, N_HEADS, V_DIM)


def dwconv3(h, w, b):
    hp = jnp.pad(h, ((0, 0), (1, 1), (0, 0)))
    return hp[:, :-2] * w[0] + hp[:, 1:-1] * w[1] + hp[:, 2:] * w[2] + b


def token_mixer(hn, l, P, angs, ctx_k, ctx_v):
    bsz, t, _ = hn.shape
    proj = hn @ P['w_in'][l]
    i0 = FOURIER_WIDTH
    i1 = i0 + QK_WIDTH
    i2 = i1 + QK_WIDTH
    i3 = i2 + V_WIDTH
    f, q, k, v, gates = jnp.split(proj, [i0, i1, i2, i3], axis=-1)
    fg = f.reshape(bsz, t, FOURIER_GROUPS, FOURIER_GROUP_DIM).astype(jnp.float32)
    fr = jnp.real(jnp.fft.fft2(fg, axes=(1, 3), norm='ortho')).astype(hn.dtype)
    a_four = fr.reshape(bsz, t, FOURIER_WIDTH) @ P['w_fourier'][l]
    q = q.reshape(bsz, t, N_HEADS, 2, QK_DIM)
    k = k.reshape(bsz, t, N_HEADS, 2, QK_DIM)
    v = v.reshape(bsz, t, N_HEADS, V_DIM)
    if angs is not None:
        q = apply_axial_rope(q, angs)
        k = apply_axial_rope(k, angs)
    if ctx_k is not None:
        k_all = jnp.concatenate([ctx_k.astype(k.dtype), k], axis=1)
        v_all = jnp.concatenate([ctx_v.astype(v.dtype), v], axis=1)
    else:
        k_all, v_all = k, v
    lam_init = 0.8 - 0.6 * math.exp(-0.3 * l)
    lp = P['lam_params'][l].astype(jnp.float32)
    lam = jnp.exp(jnp.sum(lp[0] * lp[1])) - jnp.exp(jnp.sum(lp[2] * lp[3])) + lam_init
    o = diff_attention(q, k_all, v_all, lam).astype(jnp.float32)
    o = o * lax.rsqrt(jnp.mean(o * o, axis=-1, keepdims=True) + EPS)
    o = (o * P['subln_g'][l].astype(jnp.float32) * (1.0 - lam_init)).astype(hn.dtype)
    a_attn = o.reshape(bsz, t, V_WIDTH) @ P['w_attn'][l]
    g = jax.nn.sigmoid(gates.astype(jnp.float32)).astype(hn.dtype)
    g_four, g_attn = g[..., :D_MODEL], g[..., D_MODEL:]
    out = (g_four * a_four + g_attn * a_attn) @ P['w_o'][l]
    return out, k, v


def channel_mixer(hn, l, P):
    u = hn @ P['w_up'][l]
    u = dwconv3(u, P['conv_w'][l], P['conv_b'][l])
    val, gate = u[..., :D_FF], u[..., D_FF:]
    return (jax.nn.silu(gate) * val) @ P['w_down'][l]


def trunk_layer(x, mods, l, P, angs, ctx_k, ctx_v):
    sh1, sc1, g1, sh2, sc2, g2 = mods
    hn = rmsnorm(x, P['norm1_g'][l]) * (1 + sc1) + sh1
    mix, k, v = token_mixer(hn, l, P, angs, ctx_k, ctx_v)
    x = x + g1 * mix
    hn = rmsnorm(x, P['norm2_g'][l]) * (1 + sc2) + sh2
    x = x + g2 * channel_mixer(hn, l, P)
    return x, k, v


def setup_inputs(seed: int = 0) -> dict:
    key = jax.random.key(seed)
    ks = jax.random.split(key, 24)
    f32 = jnp.float32
    nrm = lambda k, s: jax.random.normal(k, s, dtype=f32)
    D = D_MODEL
    return {
        'x_prompt': nrm(ks[0], (BATCH, SEQ, D)),
        'x_sample': nrm(ks[1], (DEC_BATCH, DEC_SEQ, D)),
        'c': nrm(ks[2], (DEC_BATCH, D)),
        'cache_k': nrm(ks[3], (DEC_BATCH, DEPTH, PAST_LEN, N_HEADS, 2, QK_DIM)),
        'cache_v': nrm(ks[4], (DEC_BATCH, DEPTH, PAST_LEN, N_HEADS, V_DIM)),
        'c_ctx': nrm(ks[5], (D,)),
        'norm1_g': 1.0 + 0.02 * nrm(ks[6], (DEPTH, D)),
        'norm2_g': 1.0 + 0.02 * nrm(ks[7], (DEPTH, D)),
        'final_g': 1.0 + 0.02 * nrm(ks[8], (D,)),
        'w_ada': 0.5 * D ** -0.5 * nrm(ks[9], (DEPTH, D, N_MOD * D)),
        'b_ada': 0.01 * nrm(ks[10], (DEPTH, N_MOD * D)),
        'w_in': D ** -0.5 * nrm(ks[11], (DEPTH, D, IN_WIDTH)),
        'w_fourier': FOURIER_WIDTH ** -0.5 * nrm(ks[12], (DEPTH, FOURIER_WIDTH, D)),
        'lam_params': 0.1 * nrm(ks[13], (DEPTH, 4, QK_DIM)),
        'subln_g': 1.0 + 0.02 * nrm(ks[14], (DEPTH, V_DIM)),
        'w_attn': V_WIDTH ** -0.5 * nrm(ks[15], (DEPTH, V_WIDTH, D)),
        'w_o': D ** -0.5 * nrm(ks[16], (DEPTH, D, D)),
        'w_up': D ** -0.5 * nrm(ks[17], (DEPTH, D, 2 * D_FF)),
        'conv_w': CONV_W ** -0.5 * nrm(ks[18], (DEPTH, CONV_W, 2 * D_FF)),
        'conv_b': 0.01 * nrm(ks[19], (DEPTH, 2 * D_FF)),
        'w_down': D_FF ** -0.5 * nrm(ks[20], (DEPTH, D_FF, D)),
    }


def reference(x_prompt, x_sample, c, cache_k, cache_v, c_ctx, norm1_g, norm2_g, final_g,
              w_ada, b_ada, w_in, w_fourier, lam_params, subln_g, w_attn, w_o,
              w_up, conv_w, conv_b, w_down):
    P = {'norm1_g': norm1_g, 'norm2_g': norm2_g, 'w_in': w_in, 'w_fourier': w_fourier,
         'lam_params': lam_params, 'subln_g': subln_g, 'w_attn': w_attn, 'w_o': w_o,
         'w_up': w_up, 'conv_w': conv_w, 'conv_b': conv_b, 'w_down': w_down}
    h = x_prompt
    ks_list, vs_list = [], []
    for l in range(DEPTH):
        mods = adaln(c_ctx[None, :], w_ada[l], b_ada[l])
        h, k, v = trunk_layer(h, mods, l, P, None, None, None)
        ks_list.append(k)
        vs_list.append(v)
    y_prompt = rmsnorm(h, final_g)
    state_k = jnp.stack(ks_list, axis=1)
    state_v = jnp.stack(vs_list, axis=1)
    angs = axial_rope_angles(x_sample.shape[1])
    z = x_sample
    for l in range(DEPTH):
        mods = adaln(c, w_ada[l], b_ada[l])
        z, _, _ = trunk_layer(z, mods, l, P, angs, cache_k[:, l], cache_v[:, l])
    y_sample = rmsnorm(z, final_g)
    return (y_prompt, y_sample, state_k, state_v)
```

```python
import functools
import math

import jax
import jax.numpy as jnp
from jax import lax
from jax.experimental import pallas as pl
from jax.experimental.pallas import tpu as pltpu

D_MODEL = 1024
BATCH = 32
SEQ = 256
DEPTH = 4
DEC_BATCH = 2
DEC_SEQ = 4096
PAST_LEN = 256
GRID_W = 64
QK_DIM = 64
V_DIM = 2 * QK_DIM
N_HEADS = D_MODEL // (2 * QK_DIM)
FOURIER_GROUPS = 4
FOURIER_WIDTH = D_MODEL // 2
FOURIER_GROUP_DIM = FOURIER_WIDTH // FOURIER_GROUPS
QK_WIDTH = N_HEADS * 2 * QK_DIM
V_WIDTH = N_HEADS * V_DIM
GATE_WIDTH = 2 * D_MODEL
FQKV_WIDTH = FOURIER_WIDTH + 2 * QK_WIDTH + V_WIDTH
D_FF = ((8 * D_MODEL // 3 + 127) // 128) * 128
ROPE_BASE = 10000.0
EPS = 1e-6
N_MOD = 6
SCALE = QK_DIM ** -0.5

BF = jnp.bfloat16
F32 = jnp.float32

MIB = 1024 * 1024
TM = 512
HALO = 16
FF_CHUNK = 256
TQ = 256
N_SIDE = 64
S1_COLS = 4
S2_ROWS = 8


def _params(n_axes, vmem_mib):
    return pltpu.CompilerParams(dimension_semantics=("arbitrary",) * n_axes,
                                vmem_limit_bytes=vmem_mib * MIB)


def _resident(shape):
    zeros = (0,) * len(shape)
    return pl.BlockSpec(shape, lambda *_: zeros, pipeline_mode=pl.Buffered(1))


def _dot(a, b):
    return jnp.dot(a, b, preferred_element_type=F32)


def _dot_nt(a, b):
    return lax.dot_general(a, b, (((1,), (1,)), ((), ())), preferred_element_type=F32)


def _norm_mod(x, g, sc, sh):
    y = x * lax.rsqrt(jnp.mean(x * x, axis=-1, keepdims=True) + EPS)
    return (y * g) * (1.0 + sc) + sh


def _adaln_kernel(c_ref, w_ref, b_ref, o_ref):
    c = c_ref[...]
    s = (c * jax.nn.sigmoid(c)).astype(BF)
    o_ref[0] = _dot(s, w_ref[0].astype(BF)) + b_ref[0]


def _adaln(cond, w_ada, b_ada):
    rows = cond.shape[0]
    return pl.pallas_call(
        _adaln_kernel,
        grid=(DEPTH, N_MOD),
        in_specs=[pl.BlockSpec((rows, D_MODEL), lambda l, j: (0, 0)),
                  pl.BlockSpec((1, D_MODEL, D_MODEL), lambda l, j: (l, 0, j)),
                  pl.BlockSpec((1, 1, D_MODEL), lambda l, j: (l, 0, j))],
        out_specs=pl.BlockSpec((1, rows, D_MODEL), lambda l, j: (l, 0, j)),
        out_shape=jax.ShapeDtypeStruct((DEPTH, rows, N_MOD * D_MODEL), F32),
        compiler_params=_params(2, 32),
        name="adaln",
    )(cond, w_ada, b_ada.reshape(DEPTH, 1, N_MOD * D_MODEL))


def _prenorm_kernel(x_ref, g_ref, mods_ref, hn_ref):
    m = mods_ref[0]
    hn_ref[...] = _norm_mod(x_ref[...], g_ref[...], m[1:2], m[0:1]).astype(BF)


def _mods_spec(n_rows, seq_len):
    if n_rows == 1:
        return pl.BlockSpec((1, N_MOD, D_MODEL), lambda i: (0, 0, 0))
    return pl.BlockSpec((1, N_MOD, D_MODEL), lambda i: ((i * TM) // seq_len, 0, 0))


def _prenorm(x, g, mods, seq_len):
    m_rows = x.shape[0]
    row = pl.BlockSpec((TM, D_MODEL), lambda i: (i, 0))
    return pl.pallas_call(
        _prenorm_kernel,
        grid=(m_rows // TM,),
        in_specs=[row, _resident((1, D_MODEL)), _mods_spec(mods.shape[0], seq_len)],
        out_specs=row,
        out_shape=jax.ShapeDtypeStruct((m_rows, D_MODEL), BF),
        compiler_params=_params(1, 32),
        name="prenorm",
    )(x, g, mods)


def _channel_dft(f, cs_ref, y_ref):
    gd = FOURIER_GROUP_DIM
    for g in range(FOURIER_GROUPS):
        yg = _dot(f[:, g * gd:(g + 1) * gd], cs_ref[...])
        y_ref[:, g * gd:(g + 1) * gd] = yg[:, :gd].astype(BF)
        y_ref[:, FOURIER_WIDTH + g * gd:FOURIER_WIDTH + (g + 1) * gd] = yg[:, gd:].astype(BF)


def _rope(x, cos, sin_lo, sin_hi):
    cols = []
    for j in range(x.shape[1] // 128):
        xb = x[:, j * 128:(j + 1) * 128]
        cols.append(xb * cos + pltpu.roll(xb, 16, 1) * sin_hi + pltpu.roll(xb, 112, 1) * sin_lo)
    return jnp.concatenate(cols, axis=1)


def _in_proj_prompt_kernel(hn_ref, w_ref, cs_ref, sk_any, sv_any, y_ref, q_ref, sk_ref, sv_ref):
    del sk_any, sv_any
    hn = hn_ref[...]
    o = FOURIER_WIDTH
    f = _dot(hn, w_ref[:, 0:o]).astype(BF)
    _channel_dft(f, cs_ref, y_ref)
    q_ref[...] = (_dot(hn, w_ref[:, o:o + QK_WIDTH]) * SCALE).astype(BF)
    o += QK_WIDTH
    sk_ref[...] = _dot(hn, w_ref[:, o:o + QK_WIDTH]).reshape(sk_ref.shape)
    o += QK_WIDTH
    sv_ref[...] = _dot(hn, w_ref[:, o:o + V_WIDTH]).reshape(sv_ref.shape)


def _in_proj_prompt(hn, w_fqkv, cs, state_k, state_v, layer):
    m_rows = hn.shape[0]
    nb = TM // SEQ
    row = lambda w: pl.BlockSpec((TM, w), lambda i: (i, 0))
    st = pl.BlockSpec((nb, 1, SEQ, D_MODEL), lambda i: (i, layer, 0, 0))
    any_spec = pl.BlockSpec(memory_space=pl.ANY)
    return pl.pallas_call(
        _in_proj_prompt_kernel,
        grid=(m_rows // TM,),
        in_specs=[row(D_MODEL), _resident(w_fqkv.shape), _resident(cs.shape), any_spec, any_spec],
        out_specs=[row(D_MODEL), row(QK_WIDTH), st, st],
        out_shape=[jax.ShapeDtypeStruct((m_rows, D_MODEL), BF),
                   jax.ShapeDtypeStruct((m_rows, QK_WIDTH), BF),
                   jax.ShapeDtypeStruct(state_k.shape, F32),
                   jax.ShapeDtypeStruct(state_v.shape, F32)],
        input_output_aliases={3: 2, 4: 3},
        compiler_params=_params(1, 48),
        name="in_proj_prompt",
    )(hn, w_fqkv, cs, state_k, state_v)


def _in_proj_sample_kernel(hn_ref, w_ref, cs_ref, cos_ref, slo_ref, shi_ref, y_ref, q_ref, k_ref, v_ref):
    hn = hn_ref[...]
    o = FOURIER_WIDTH
    f = _dot(hn, w_ref[:, 0:o]).astype(BF)
    _channel_dft(f, cs_ref, y_ref)
    cos, slo, shi = cos_ref[...], slo_ref[...], shi_ref[...]
    q = _rope(_dot(hn, w_ref[:, o:o + QK_WIDTH]), cos, slo, shi)
    q_ref[...] = (q * SCALE).astype(BF)
    o += QK_WIDTH
    k_ref[...] = _rope(_dot(hn, w_ref[:, o:o + QK_WIDTH]), cos, slo, shi).astype(BF)
    o += QK_WIDTH
    v_ref[...] = _dot(hn, w_ref[:, o:o + V_WIDTH]).astype(BF)


def _in_proj_sample(hn, w_fqkv, cs, rope_tabs):
    m_rows = hn.shape[0]
    row = lambda w: pl.BlockSpec((TM, w), lambda i: (i, 0))
    tab = pl.BlockSpec((TM, 128), lambda i: (i % (DEC_SEQ // TM), 0))
    return pl.pallas_call(
        _in_proj_sample_kernel,
        grid=(m_rows // TM,),
        in_specs=[row(D_MODEL), _resident(w_fqkv.shape), _resident(cs.shape), tab, tab, tab],
        out_specs=[row(D_MODEL), row(QK_WIDTH), row(QK_WIDTH), row(V_WIDTH)],
        out_shape=[jax.ShapeDtypeStruct((m_rows, D_MODEL), BF),
                   jax.ShapeDtypeStruct((m_rows, QK_WIDTH), BF),
                   jax.ShapeDtypeStruct((m_rows, QK_WIDTH), BF),
                   jax.ShapeDtypeStruct((m_rows, V_WIDTH), BF)],
        compiler_params=_params(1, 48),
        name="in_proj_sample",
    )(hn, w_fqkv, cs, *rope_tabs)


def _fourier_prompt_kernel(y_ref, ct_ref, st_ref, fr_ref):
    y = y_ref[...]
    fr = _dot(ct_ref[...], y[:, :FOURIER_WIDTH]) + _dot(st_ref[...], y[:, FOURIER_WIDTH:])
    fr_ref[...] = fr.astype(BF)


def _fourier_prompt(y, ct, st):
    m_rows = y.shape[0]
    return pl.pallas_call(
        _fourier_prompt_kernel,
        grid=(m_rows // SEQ,),
        in_specs=[pl.BlockSpec((SEQ, D_MODEL), lambda i: (i, 0)), _resident(ct.shape), _resident(st.shape)],
        out_specs=pl.BlockSpec((SEQ, FOURIER_WIDTH), lambda i: (i, 0)),
        out_shape=jax.ShapeDtypeStruct((m_rows, FOURIER_WIDTH), BF),
        compiler_params=_params(1, 32),
        name="fourier_prompt",
    )(y, ct, st)


def _fourier_stage1_kernel(y_ref, ff_ref, a_ref):
    pq = _dot(ff_ref[...], y_ref[0])
    p, q = pq[:N_SIDE], pq[N_SIDE:]
    w = FOURIER_WIDTH
    for s in range(S1_COLS):
        c = s * 2 * w
        a_ref[0, :, c:c + w] = (p[:, c:c + w] + q[:, c + w:c + 2 * w]).astype(BF)
        a_ref[0, :, c + w:c + 2 * w] = (p[:, c + w:c + 2 * w] - q[:, c:c + w]).astype(BF)


def _fourier_stage2_kernel(a_ref, gc_ref, gs_ref, fr_ref):
    w = FOURIER_WIDTH
    for kk in range(S2_ROWS):
        a = a_ref[0, kk * N_SIDE:(kk + 1) * N_SIDE, :]
        xr = _dot(gc_ref[kk], a[:, :w]) + _dot(gs_ref[kk], a[:, w:])
        fr_ref[0, :, kk * w:(kk + 1) * w] = xr.astype(BF)


def _fourier_sample(y, ff, gc, gs):
    nb = y.shape[0] // DEC_SEQ
    row_len = N_SIDE * D_MODEL
    y3 = y.reshape(nb, N_SIDE, row_len)
    cols = S1_COLS * D_MODEL
    a = pl.pallas_call(
        _fourier_stage1_kernel,
        grid=(nb, row_len // cols),
        in_specs=[pl.BlockSpec((1, N_SIDE, cols), lambda b, j: (b, 0, j)),
                  pl.BlockSpec(ff.shape, lambda b, j: (0, 0))],
        out_specs=pl.BlockSpec((1, N_SIDE, cols), lambda b, j: (b, 0, j)),
        out_shape=jax.ShapeDtypeStruct((nb, N_SIDE, row_len), BF),
        compiler_params=_params(2, 32),
        name="fourier_stage1",
    )(y3, ff)
    a = a.reshape(nb, N_SIDE * N_SIDE, D_MODEL)
    fr = pl.pallas_call(
        _fourier_stage2_kernel,
        grid=(nb, N_SIDE // S2_ROWS),
        in_specs=[pl.BlockSpec((1, S2_ROWS * N_SIDE, D_MODEL), lambda b, j: (b, j, 0)),
                  pl.BlockSpec((S2_ROWS, N_SIDE, N_SIDE), lambda b, j: (j, 0, 0)),
                  pl.BlockSpec((S2_ROWS, N_SIDE, N_SIDE), lambda b, j: (j, 0, 0))],
        out_specs=pl.BlockSpec((1, N_SIDE, S2_ROWS * FOURIER_WIDTH), lambda b, j: (b, 0, j)),
        out_shape=jax.ShapeDtypeStruct((nb, N_SIDE, N_SIDE * FOURIER_WIDTH), BF),
        compiler_params=_params(2, 32),
        name="fourier_stage2",
    )(a, gc, gs)
    return fr.reshape(nb * DEC_SEQ, FOURIER_WIDTH)


def _lam(lam_ref, lam_init):
    lp = lam_ref[0]
    a = jnp.sum(lp[0:1] * lp[1:2], axis=(0, 1), keepdims=True)
    b = jnp.sum(lp[2:3] * lp[3:4], axis=(0, 1), keepdims=True)
    return jnp.exp(a) - jnp.exp(b) + lam_init


def _split_maps(q):
    lane = lax.broadcasted_iota(jnp.int32, q.shape, 1)
    zero = jnp.zeros_like(q)
    return jnp.where(lane < QK_DIM, q, zero), jnp.where(lane >= QK_DIM, q, zero)


def _sub_norm(o, g, lam_init):
    o = o * lax.rsqrt(jnp.mean(o * o, axis=-1, keepdims=True) + EPS)
    return o * g * (1.0 - lam_init)


def _attn_prompt_kernel(q_ref, k_ref, v_ref, lam_ref, g_ref, o_ref, *, lam_init):
    lam = _lam(lam_ref, lam_init)
    g = g_ref[0]
    for h in range(N_HEADS):
        sl = slice(h * V_DIM, (h + 1) * V_DIM)
        q1, q2 = _split_maps(q_ref[:, sl])
        k = k_ref[0, 0, :, sl].astype(BF)
        v = v_ref[0, 0, :, sl].astype(BF)
        p1 = jax.nn.softmax(_dot_nt(q1, k), axis=-1)
        p2 = jax.nn.softmax(_dot_nt(q2, k), axis=-1)
        o = _dot((p1 - lam * p2).astype(BF), v)
        o_ref[:, sl] = _sub_norm(o, g, lam_init).astype(BF)


def _attn_prompt(q, state_k, state_v, lam_params, subln_g, layer):
    m_rows = q.shape[0]
    lam_init = 0.8 - 0.6 * math.exp(-0.3 * layer)
    st = pl.BlockSpec((1, 1, SEQ, D_MODEL), lambda b: (b, layer, 0, 0))
    return pl.pallas_call(
        functools.partial(_attn_prompt_kernel, lam_init=lam_init),
        grid=(m_rows // SEQ,),
        in_specs=[pl.BlockSpec((SEQ, QK_WIDTH), lambda b: (b, 0)), st, st,
                  pl.BlockSpec((1, 4, QK_DIM), lambda b: (layer, 0, 0)),
                  pl.BlockSpec((1, 1, V_DIM), lambda b: (layer, 0, 0))],
        out_specs=pl.BlockSpec((SEQ, V_WIDTH), lambda b: (b, 0)),
        out_shape=jax.ShapeDtypeStruct((m_rows, V_WIDTH), BF),
        compiler_params=_params(1, 32),
        name="attn_prompt",
    )(q, state_k, state_v, lam_params, subln_g.reshape(DEPTH, 1, V_DIM))


def _attn_sample_kernel(q_ref, kn_ref, vn_ref, kc_ref, vc_ref, lam_ref, g_ref, o_ref, *, lam_init):
    lam = _lam(lam_ref, lam_init)
    q1, q2 = _split_maps(q_ref[...])
    kn = kn_ref[...]
    kc = kc_ref[0, 0].astype(BF)

    def probs(qm):
        sn = _dot_nt(qm, kn)
        sc = _dot_nt(qm, kc)
        m = jnp.maximum(jnp.max(sn, axis=-1, keepdims=True), jnp.max(sc, axis=-1, keepdims=True))
        pn = jnp.exp(sn - m)
        pc = jnp.exp(sc - m)
        r = 1.0 / (jnp.sum(pn, axis=-1, keepdims=True) + jnp.sum(pc, axis=-1, keepdims=True))
        return pn, pc, r

    p1n, p1c, r1 = probs(q1)
    p2n, p2c, r2 = probs(q2)
    r2 = lam * r2
    o = _dot((p1n * r1 - p2n * r2).astype(BF), vn_ref[...])
    o += _dot((p1c * r1 - p2c * r2).astype(BF), vc_ref[0, 0].astype(BF))
    o_ref[...] = _sub_norm(o, g_ref[0], lam_init).astype(BF)


def _attn_sample(q, k, v, cache_k, cache_v, lam_params, subln_g, layer):
    m_rows = q.shape[0]
    nq = DEC_SEQ // TQ
    lam_init = 0.8 - 0.6 * math.exp(-0.3 * layer)
    blk = pl.BlockSpec((TQ, V_DIM), lambda b, h, i: (b * nq + i, h))
    new = pl.BlockSpec((DEC_SEQ, V_DIM), lambda b, h, i: (b, h))
    old = pl.BlockSpec((1, 1, PAST_LEN, V_DIM), lambda b, h, i: (b, layer, 0, h))
    return pl.pallas_call(
        functools.partial(_attn_sample_kernel, lam_init=lam_init),
        grid=(m_rows // DEC_SEQ, N_HEADS, nq),
        in_specs=[blk, new, new, old, old,
                  pl.BlockSpec((1, 4, QK_DIM), lambda b, h, i: (layer, 0, 0)),
                  pl.BlockSpec((1, 1, V_DIM), lambda b, h, i: (layer, 0, 0))],
        out_specs=blk,
        out_shape=jax.ShapeDtypeStruct((m_rows, V_WIDTH), BF),
        compiler_params=_params(3, 56),
        name="attn_sample",
    )(q, k, v, cache_k, cache_v, lam_params, subln_g.reshape(DEPTH, 1, V_DIM))


def _mix_out_kernel(hn_ref, fr_ref, o_ref, x_ref, wg_ref, wf_ref, wa_ref, wo_ref, mods_ref, g2_ref,
                    xo_ref, hno_ref):
    hn = hn_ref[...]
    a_four = _dot(fr_ref[...], wf_ref[...])
    mixed = jax.nn.sigmoid(_dot(hn, wg_ref[:, :D_MODEL])) * a_four
    a_attn = _dot(o_ref[...], wa_ref[...])
    mixed += jax.nn.sigmoid(_dot(hn, wg_ref[:, D_MODEL:])) * a_attn
    mix = _dot(mixed.astype(BF), wo_ref[...])
    m = mods_ref[0]
    x = x_ref[...] + m[2:3] * mix
    xo_ref[...] = x
    hno_ref[...] = _norm_mod(x, g2_ref[...], m[4:5], m[3:4]).astype(BF)


def _mix_out(hn, fr, o, x, w_gate, w_four, w_attn, w_o, mods, norm2_g, seq_len):
    m_rows = hn.shape[0]
    row = lambda w: pl.BlockSpec((TM, w), lambda i: (i, 0))
    return pl.pallas_call(
        _mix_out_kernel,
        grid=(m_rows // TM,),
        in_specs=[row(D_MODEL), row(FOURIER_WIDTH), row(V_WIDTH), row(D_MODEL),
                  _resident(w_gate.shape), _resident(w_four.shape), _resident(w_attn.shape),
                  _resident(w_o.shape), _mods_spec(mods.shape[0], seq_len), _resident((1, D_MODEL))],
        out_specs=[row(D_MODEL), row(D_MODEL)],
        out_shape=[jax.ShapeDtypeStruct((m_rows, D_MODEL), F32),
                   jax.ShapeDtypeStruct((m_rows, D_MODEL), BF)],
        compiler_params=_params(1, 56),
        name="mix_out",
    )(hn, fr, o, x, w_gate, w_four, w_attn, w_o, mods, norm2_g)


def _ffn_kernel(hp_ref, hn_ref, hx_ref, x_ref, wup_ref, cw_ref, cb_ref, wdn_ref, mods_ref, gn_ref,
                modsn_ref, *rest, seq_len, last):
    if last:
        y_ref, lhs_scr, u_scr, h_scr = rest
    else:
        xo_ref, hno_ref, lhs_scr, u_scr, h_scr = rest
    lhs_scr[0:HALO] = hp_ref[...]
    lhs_scr[HALO:HALO + TM] = hn_ref[...]
    lhs_scr[HALO + TM:] = hx_ref[...]
    lhs = lhs_scr[...]
    pos = (pl.program_id(0) * TM + lax.broadcasted_iota(jnp.int32, (TM, 1), 0)) % seq_len
    has_prev = pos != 0
    has_next = pos != seq_len - 1

    def conv(col, slot):
        u_scr[slot] = _dot(lhs, wup_ref[:, col:col + FF_CHUNK])
        prev = jnp.where(has_prev, u_scr[slot, pl.ds(HALO - 1, TM), :], 0.0)
        cur = u_scr[slot, pl.ds(HALO, TM), :]
        nxt = jnp.where(has_next, u_scr[slot, pl.ds(HALO + 1, TM), :], 0.0)
        w = cw_ref[:, col:col + FF_CHUNK]
        return prev * w[0:1] + cur * w[1:2] + nxt * w[2:3] + cb_ref[:, col:col + FF_CHUNK]

    for c in range(D_FF // FF_CHUNK):
        val = conv(c * FF_CHUNK, 0)
        gate = conv(D_FF + c * FF_CHUNK, 1)
        h_scr[:, c * FF_CHUNK:(c + 1) * FF_CHUNK] = (gate * jax.nn.sigmoid(gate) * val).astype(BF)

    m = mods_ref[0]
    x = x_ref[...] + m[5:6] * _dot(h_scr[...], wdn_ref[...])
    if last:
        y_ref[...] = x * lax.rsqrt(jnp.mean(x * x, axis=-1, keepdims=True) + EPS) * gn_ref[...]
    else:
        xo_ref[...] = x
        mn = modsn_ref[0]
        hno_ref[...] = _norm_mod(x, gn_ref[...], mn[1:2], mn[0:1]).astype(BF)


def _ffn(hn, x, w_up, conv_w, conv_b, w_down, mods, g_next, mods_next, seq_len, last):
    m_rows = hn.shape[0]
    per = TM // HALO
    row = lambda w: pl.BlockSpec((TM, w), lambda i: (i, 0))
    prev = pl.BlockSpec((HALO, D_MODEL), lambda i: (jnp.maximum(i * per - 1, 0), 0))
    nxt = pl.BlockSpec((HALO, D_MODEL), lambda i: (jnp.minimum((i + 1) * per, m_rows // HALO - 1), 0))
    if last:
        out_specs = row(D_MODEL)
        out_shape = jax.ShapeDtypeStruct((m_rows, D_MODEL), F32)
    else:
        out_specs = [row(D_MODEL), row(D_MODEL)]
        out_shape = [jax.ShapeDtypeStruct((m_rows, D_MODEL), F32),
                     jax.ShapeDtypeStruct((m_rows, D_MODEL), BF)]
    return pl.pallas_call(
        functools.partial(_ffn_kernel, seq_len=seq_len, last=last),
        grid=(m_rows // TM,),
        in_specs=[prev, row(D_MODEL), nxt, row(D_MODEL),
                  _resident(w_up.shape), _resident(conv_w.shape), _resident(conv_b.shape),
                  _resident(w_down.shape), _mods_spec(mods.shape[0], seq_len),
                  _resident((1, D_MODEL)), _mods_spec(mods_next.shape[0], seq_len)],
        out_specs=out_specs,
        out_shape=out_shape,
        scratch_shapes=[pltpu.VMEM((TM + 2 * HALO, D_MODEL), BF),
                        pltpu.VMEM((2, TM + 2 * HALO, FF_CHUNK), F32),
                        pltpu.VMEM((TM, D_FF), BF)],
        compiler_params=_params(1, 56),
        name="ffn",
    )(hn, hn, hn, x, w_up, conv_w, conv_b, w_down, mods, g_next, mods_next)


def _cos_sin(num, den):
    ang = (2.0 * math.pi / den) * (num % den).astype(F32)
    return jnp.cos(ang), jnp.sin(ang)


def _dft_tables():
    n = jnp.arange(FOURIER_GROUP_DIM)
    c, s = _cos_sin(n[:, None] * n[None, :], FOURIER_GROUP_DIM)
    cs = (jnp.concatenate([c, -s], axis=1) * FOURIER_GROUP_DIM ** -0.5).astype(BF)
    t = jnp.arange(SEQ)
    c, s = _cos_sin(t[:, None] * t[None, :], SEQ)
    ct, st = (c * SEQ ** -0.5).astype(BF), (s * SEQ ** -0.5).astype(BF)
    r = jnp.arange(N_SIDE)
    c, s = _cos_sin(r[:, None] * r[None, :], N_SIDE)
    ff = (jnp.concatenate([c, s], axis=0) * N_SIDE ** -0.5).astype(BF)
    k = r[:, None, None] + N_SIDE * r[None, :, None]
    c, s = _cos_sin(k * r[None, None, :], N_SIDE * N_SIDE)
    gc, gs = (c * N_SIDE ** -0.5).astype(BF), (s * N_SIDE ** -0.5).astype(BF)
    return cs, ct, st, ff, gc, gs


def _rope_tables():
    half = QK_DIM // 2
    inv_freq = 1.0 / (ROPE_BASE ** (jnp.arange(0, half, 2, dtype=F32) / half))
    t = jnp.arange(DEC_SEQ)
    row = (t // GRID_W).astype(F32)
    col = (t % GRID_W).astype(F32)
    lane = jnp.arange(128)
    d = lane % QK_DIM
    freq = inv_freq[d % (half // 2)]
    ang = jnp.where((d < half)[None, :], row[:, None] * freq[None, :], col[:, None] * freq[None, :])
    cos, sin = jnp.cos(ang), jnp.sin(ang)
    upper = ((d % half) >= half // 2)[None, :]
    sin_hi = jnp.where(upper, sin, 0.0)
    sin_lo = jnp.where(upper, 0.0, -sin)
    return cos, sin_lo, sin_hi


def _trunk(x, mods, w, seq_len, layer_fns):
    in_proj, fourier, attention = layer_fns
    hn = _prenorm(x, w["norm1_g"][0:1], mods[0], seq_len)
    for l in range(DEPTH):
        y, q, kv = in_proj(hn, l)
        fr = fourier(y)
        o = attention(q, kv, l)
        x, hn2 = _mix_out(hn, fr, o, x, w["w_gate"][l], w["w_fourier"][l], w["w_attn"][l], w["w_o"][l],
                          mods[l], w["norm2_g"][l:l + 1], seq_len)
        last = l == DEPTH - 1
        g_next = w["final_g"] if last else w["norm1_g"][l + 1:l + 2]
        res = _ffn(hn2, x, w["w_up"][l], w["conv_w"][l], w["conv_b"][l:l + 1], w["w_down"][l],
                   mods[l], g_next, mods[l if last else l + 1], seq_len, last)
        if last:
            return res
        x, hn = res


def kernel(x_prompt, x_sample, c, cache_k, cache_v, c_ctx, norm1_g, norm2_g, final_g, w_ada, b_ada, w_in,
           w_fourier, lam_params, subln_g, w_attn, w_o, w_up, conv_w, conv_b, w_down):
    w_in_b = w_in.astype(BF)
    w = {
        "norm1_g": norm1_g, "norm2_g": norm2_g, "final_g": final_g.reshape(1, D_MODEL),
        "w_fqkv": w_in_b[:, :, :FQKV_WIDTH], "w_gate": w_in_b[:, :, FQKV_WIDTH:],
        "w_fourier": w_fourier.astype(BF), "w_attn": w_attn.astype(BF), "w_o": w_o.astype(BF),
        "w_up": w_up.astype(BF), "conv_w": conv_w, "conv_b": conv_b, "w_down": w_down.astype(BF),
    }
    cond = jnp.concatenate([c_ctx[None, :], c, jnp.zeros((8 - 1 - DEC_BATCH, D_MODEL), F32)], axis=0)
    mods = _adaln(cond, w_ada, b_ada)[:, :1 + DEC_BATCH].reshape(DEPTH, 1 + DEC_BATCH, N_MOD, D_MODEL)
    cs, ct, st, ff, gc, gs = _dft_tables()
    rope_tabs = _rope_tables()

    state = {"k": jnp.zeros((BATCH, DEPTH, SEQ, QK_WIDTH), F32),
             "v": jnp.zeros((BATCH, DEPTH, SEQ, V_WIDTH), F32)}

    def in_proj_p(hn, l):
        y, q, state["k"], state["v"] = _in_proj_prompt(hn, w["w_fqkv"][l], cs, state["k"], state["v"], l)
        return y, q, None

    y_prompt = _trunk(
        x_prompt.reshape(BATCH * SEQ, D_MODEL), mods[:, 0:1], w, SEQ,
        (in_proj_p,
         lambda y: _fourier_prompt(y, ct, st),
         lambda q, kv, l: _attn_prompt(q, state["k"], state["v"], lam_params, subln_g, l)))

    ck = cache_k.reshape(DEC_BATCH, DEPTH, PAST_LEN, QK_WIDTH)
    cv = cache_v.reshape(DEC_BATCH, DEPTH, PAST_LEN, V_WIDTH)

    def in_proj_s(hn, l):
        y, q, k, v = _in_proj_sample(hn, w["w_fqkv"][l], cs, rope_tabs)
        return y, q, (k, v)

    y_sample = _trunk(
        x_sample.reshape(DEC_BATCH * DEC_SEQ, D_MODEL), mods[:, 1:], w, DEC_SEQ,
        (in_proj_s,
         lambda y: _fourier_sample(y, ff, gc, gs),
         lambda q, kv, l: _attn_sample(q, kv[0], kv[1], ck, cv, lam_params, subln_g, l)))

    return (y_prompt.reshape(BATCH, SEQ, D_MODEL),
            y_sample.reshape(DEC_BATCH, DEC_SEQ, D_MODEL),
            state["k"].reshape(BATCH, DEPTH, SEQ, N_HEADS, 2, QK_DIM),
            state["v"].reshape(BATCH, DEPTH, SEQ, N_HEADS, V_DIM))
```

```python
import functools
import math

import jax
import jax.numpy as jnp
from jax import lax
from jax.experimental import pallas as pl
from jax.experimental.pallas import tpu as pltpu

D_MODEL = 1024
BATCH = 32
SEQ = 256
DEPTH = 4
DEC_BATCH = 2
DEC_SEQ = 4096
PAST_LEN = 256
GRID_W = 64
QK_DIM = 64
V_DIM = 2 * QK_DIM
N_HEADS = D_MODEL // (2 * QK_DIM)
FOURIER_GROUPS = 4
FOURIER_WIDTH = D_MODEL // 2
FOURIER_GROUP_DIM = FOURIER_WIDTH // FOURIER_GROUPS
QK_WIDTH = N_HEADS * 2 * QK_DIM
V_WIDTH = N_HEADS * V_DIM
GATE_WIDTH = 2 * D_MODEL
FQKV_WIDTH = FOURIER_WIDTH + 2 * QK_WIDTH + V_WIDTH
D_FF = ((8 * D_MODEL // 3 + 127) // 128) * 128
ROPE_BASE = 10000.0
EPS = 1e-6
N_MOD = 6
SCALE = QK_DIM ** -0.5
LOG2E = math.log2(math.e)

BF = jnp.bfloat16
F32 = jnp.float32

MIB = 1024 * 1024
TM = 512
HALO = 16
FF_CHUNK = 256
ATT_Q = 256
ATT_CK = 512
N_SIDE = 64
S1_COLS = 4
S2_ROWS = 8


def _params(n_axes, vmem_mib, flags=None):
    return pltpu.CompilerParams(dimension_semantics=("arbitrary",) * n_axes,
                                vmem_limit_bytes=vmem_mib * MIB, flags=flags)


def _resident(shape):
    zeros = (0,) * len(shape)
    return pl.BlockSpec(shape, lambda *_: zeros, pipeline_mode=pl.Buffered(1))


def _dot(a, b):
    return jnp.dot(a, b, preferred_element_type=F32)


def _dot_nt(a, b):
    return lax.dot_general(a, b, (((1,), (1,)), ((), ())), preferred_element_type=F32)


def _norm_mod(x, g, sc, sh):
    y = x * lax.rsqrt(jnp.mean(x * x, axis=-1, keepdims=True) + EPS)
    return (y * g) * (1.0 + sc) + sh


def _adaln_kernel(c_ref, w_ref, b_ref, o_ref):
    c = c_ref[...]
    s = (c * jax.nn.sigmoid(c)).astype(BF)
    o_ref[0] = _dot(s, w_ref[0].astype(BF)) + b_ref[0]


def _adaln(cond, w_ada, b_ada):
    rows = cond.shape[0]
    return pl.pallas_call(
        _adaln_kernel,
        grid=(DEPTH, N_MOD),
        in_specs=[pl.BlockSpec((rows, D_MODEL), lambda l, j: (0, 0)),
                  pl.BlockSpec((1, D_MODEL, D_MODEL), lambda l, j: (l, 0, j)),
                  pl.BlockSpec((1, 1, D_MODEL), lambda l, j: (l, 0, j))],
        out_specs=pl.BlockSpec((1, rows, D_MODEL), lambda l, j: (l, 0, j)),
        out_shape=jax.ShapeDtypeStruct((DEPTH, rows, N_MOD * D_MODEL), F32),
        compiler_params=_params(2, 32),
        name="adaln",
    )(cond, w_ada, b_ada.reshape(DEPTH, 1, N_MOD * D_MODEL))


def _prenorm_kernel(x_ref, g_ref, mods_ref, hn_ref):
    m = mods_ref[0]
    hn_ref[...] = _norm_mod(x_ref[...], g_ref[...], m[1:2], m[0:1]).astype(BF)


def _mods_spec(n_rows, seq_len):
    if n_rows == 1:
        return pl.BlockSpec((1, N_MOD, D_MODEL), lambda i: (0, 0, 0))
    return pl.BlockSpec((1, N_MOD, D_MODEL), lambda i: ((i * TM) // seq_len, 0, 0))


def _prenorm(x, g, mods, seq_len):
    m_rows = x.shape[0]
    row = pl.BlockSpec((TM, D_MODEL), lambda i: (i, 0))
    return pl.pallas_call(
        _prenorm_kernel,
        grid=(m_rows // TM,),
        in_specs=[row, _resident((1, D_MODEL)), _mods_spec(mods.shape[0], seq_len)],
        out_specs=row,
        out_shape=jax.ShapeDtypeStruct((m_rows, D_MODEL), BF),
        compiler_params=_params(1, 32),
        name="prenorm",
    )(x, g, mods)


def _channel_dft(f, cs_ref, y_ref):
    gd = FOURIER_GROUP_DIM
    for g in range(FOURIER_GROUPS):
        yg = _dot(f[:, g * gd:(g + 1) * gd], cs_ref[...])
        y_ref[:, g * gd:(g + 1) * gd] = yg[:, :gd].astype(BF)
        y_ref[:, FOURIER_WIDTH + g * gd:FOURIER_WIDTH + (g + 1) * gd] = yg[:, gd:].astype(BF)


def _rope(x, cos, sin_lo, sin_hi):
    cols = []
    for j in range(x.shape[1] // 128):
        xb = x[:, j * 128:(j + 1) * 128]
        cols.append(xb * cos + pltpu.roll(xb, 16, 1) * sin_hi + pltpu.roll(xb, 112, 1) * sin_lo)
    return jnp.concatenate(cols, axis=1)


def _in_proj_prompt_kernel(hn_ref, w_ref, cs_ref, sk_any, sv_any, y_ref, q_ref, sk_ref, sv_ref):
    del sk_any, sv_any
    hn = hn_ref[...]
    o = FOURIER_WIDTH
    f = _dot(hn, w_ref[:, 0:o]).astype(BF)
    _channel_dft(f, cs_ref, y_ref)
    q_ref[...] = (_dot(hn, w_ref[:, o:o + QK_WIDTH]) * SCALE).astype(BF)
    o += QK_WIDTH
    sk_ref[...] = _dot(hn, w_ref[:, o:o + QK_WIDTH]).reshape(sk_ref.shape)
    o += QK_WIDTH
    sv_ref[...] = _dot(hn, w_ref[:, o:o + V_WIDTH]).reshape(sv_ref.shape)


def _in_proj_prompt(hn, w_fqkv, cs, state_k, state_v, layer):
    m_rows = hn.shape[0]
    nb = TM // SEQ
    row = lambda w: pl.BlockSpec((TM, w), lambda i: (i, 0))
    st = pl.BlockSpec((nb, 1, SEQ, D_MODEL), lambda i: (i, layer, 0, 0))
    any_spec = pl.BlockSpec(memory_space=pl.ANY)
    return pl.pallas_call(
        _in_proj_prompt_kernel,
        grid=(m_rows // TM,),
        in_specs=[row(D_MODEL), _resident(w_fqkv.shape), _resident(cs.shape), any_spec, any_spec],
        out_specs=[row(D_MODEL), row(QK_WIDTH), st, st],
        out_shape=[jax.ShapeDtypeStruct((m_rows, D_MODEL), BF),
                   jax.ShapeDtypeStruct((m_rows, QK_WIDTH), BF),
                   jax.ShapeDtypeStruct(state_k.shape, F32),
                   jax.ShapeDtypeStruct(state_v.shape, F32)],
        input_output_aliases={3: 2, 4: 3},
        compiler_params=_params(1, 48),
        name="in_proj_prompt",
    )(hn, w_fqkv, cs, state_k, state_v)


def _in_proj_sample_kernel(hn_ref, w_ref, cs_ref, cos_ref, slo_ref, shi_ref, y_ref, q_ref, k_ref, v_ref):
    hn = hn_ref[...]
    o = FOURIER_WIDTH
    f = _dot(hn, w_ref[:, 0:o]).astype(BF)
    _channel_dft(f, cs_ref, y_ref)
    cos, slo, shi = cos_ref[...], slo_ref[...], shi_ref[...]
    q = _rope(_dot(hn, w_ref[:, o:o + QK_WIDTH]), cos, slo, shi)
    q_ref[...] = (q * (SCALE * LOG2E)).astype(BF)
    o += QK_WIDTH
    k_ref[...] = _rope(_dot(hn, w_ref[:, o:o + QK_WIDTH]), cos, slo, shi).astype(BF)
    o += QK_WIDTH
    v_ref[...] = _dot(hn, w_ref[:, o:o + V_WIDTH]).astype(BF)


def _in_proj_sample(hn, w_fqkv, cs, rope_tabs):
    m_rows = hn.shape[0]
    row = lambda w: pl.BlockSpec((TM, w), lambda i: (i, 0))
    tab = pl.BlockSpec((TM, 128), lambda i: (i % (DEC_SEQ // TM), 0))
    return pl.pallas_call(
        _in_proj_sample_kernel,
        grid=(m_rows // TM,),
        in_specs=[row(D_MODEL), _resident(w_fqkv.shape), _resident(cs.shape), tab, tab, tab],
        out_specs=[row(D_MODEL), row(QK_WIDTH), row(QK_WIDTH), row(V_WIDTH)],
        out_shape=[jax.ShapeDtypeStruct((m_rows, D_MODEL), BF),
                   jax.ShapeDtypeStruct((m_rows, QK_WIDTH), BF),
                   jax.ShapeDtypeStruct((m_rows, QK_WIDTH), BF),
                   jax.ShapeDtypeStruct((m_rows, V_WIDTH), BF)],
        compiler_params=_params(1, 48),
        name="in_proj_sample",
    )(hn, w_fqkv, cs, *rope_tabs)


def _fourier_prompt_kernel(y_ref, ct_ref, st_ref, fr_ref):
    y = y_ref[...]
    fr = _dot(ct_ref[...], y[:, :FOURIER_WIDTH]) + _dot(st_ref[...], y[:, FOURIER_WIDTH:])
    fr_ref[...] = fr.astype(BF)


def _fourier_prompt(y, ct, st):
    m_rows = y.shape[0]
    return pl.pallas_call(
        _fourier_prompt_kernel,
        grid=(m_rows // SEQ,),
        in_specs=[pl.BlockSpec((SEQ, D_MODEL), lambda i: (i, 0)), _resident(ct.shape), _resident(st.shape)],
        out_specs=pl.BlockSpec((SEQ, FOURIER_WIDTH), lambda i: (i, 0)),
        out_shape=jax.ShapeDtypeStruct((m_rows, FOURIER_WIDTH), BF),
        compiler_params=_params(1, 32),
        name="fourier_prompt",
    )(y, ct, st)


def _fourier_stage1_kernel(y_ref, ff_ref, a_ref):
    pq = _dot(ff_ref[...], y_ref[0])
    p, q = pq[:N_SIDE], pq[N_SIDE:]
    w = FOURIER_WIDTH
    for s in range(S1_COLS):
        c = s * 2 * w
        a_ref[0, :, c:c + w] = (p[:, c:c + w] + q[:, c + w:c + 2 * w]).astype(BF)
        a_ref[0, :, c + w:c + 2 * w] = (p[:, c + w:c + 2 * w] - q[:, c:c + w]).astype(BF)


def _fourier_stage2_kernel(a_ref, gc_ref, gs_ref, fr_ref):
    w = FOURIER_WIDTH
    for kk in range(S2_ROWS):
        a = a_ref[0, kk * N_SIDE:(kk + 1) * N_SIDE, :]
        xr = _dot(gc_ref[kk], a[:, :w]) + _dot(gs_ref[kk], a[:, w:])
        fr_ref[0, :, kk * w:(kk + 1) * w] = xr.astype(BF)


def _fourier_sample(y, ff, gc, gs):
    nb = y.shape[0] // DEC_SEQ
    row_len = N_SIDE * D_MODEL
    y3 = y.reshape(nb, N_SIDE, row_len)
    cols = S1_COLS * D_MODEL
    a = pl.pallas_call(
        _fourier_stage1_kernel,
        grid=(nb, row_len // cols),
        in_specs=[pl.BlockSpec((1, N_SIDE, cols), lambda b, j: (b, 0, j)),
                  pl.BlockSpec(ff.shape, lambda b, j: (0, 0))],
        out_specs=pl.BlockSpec((1, N_SIDE, cols), lambda b, j: (b, 0, j)),
        out_shape=jax.ShapeDtypeStruct((nb, N_SIDE, row_len), BF),
        compiler_params=_params(2, 32),
        name="fourier_stage1",
    )(y3, ff)
    a = a.reshape(nb, N_SIDE * N_SIDE, D_MODEL)
    fr = pl.pallas_call(
        _fourier_stage2_kernel,
        grid=(nb, N_SIDE // S2_ROWS),
        in_specs=[pl.BlockSpec((1, S2_ROWS * N_SIDE, D_MODEL), lambda b, j: (b, j, 0)),
                  pl.BlockSpec((S2_ROWS, N_SIDE, N_SIDE), lambda b, j: (j, 0, 0)),
                  pl.BlockSpec((S2_ROWS, N_SIDE, N_SIDE), lambda b, j: (j, 0, 0))],
        out_specs=pl.BlockSpec((1, N_SIDE, S2_ROWS * FOURIER_WIDTH), lambda b, j: (b, 0, j)),
        out_shape=jax.ShapeDtypeStruct((nb, N_SIDE, N_SIDE * FOURIER_WIDTH), BF),
        compiler_params=_params(2, 32),
        name="fourier_stage2",
    )(a, gc, gs)
    return fr.reshape(nb * DEC_SEQ, FOURIER_WIDTH)


def _lam(lam_ref, lam_init):
    lp = lam_ref[0]
    a = jnp.sum(lp[0:1] * lp[1:2], axis=(0, 1), keepdims=True)
    b = jnp.sum(lp[2:3] * lp[3:4], axis=(0, 1), keepdims=True)
    return jnp.exp(a) - jnp.exp(b) + lam_init


def _split_maps(q):
    lane = lax.broadcasted_iota(jnp.int32, q.shape, 1)
    zero = jnp.zeros_like(q)
    return jnp.where(lane < QK_DIM, q, zero), jnp.where(lane >= QK_DIM, q, zero)


def _sub_norm(o, g, lam_init):
    o = o * lax.rsqrt(jnp.mean(o * o, axis=-1, keepdims=True) + EPS)
    return o * g * (1.0 - lam_init)


def _attn_prompt_kernel(q_ref, k_ref, v_ref, lam_ref, g_ref, o_ref, *, lam_init):
    lam = _lam(lam_ref, lam_init)
    g = g_ref[0]
    for h in range(N_HEADS):
        sl = slice(h * V_DIM, (h + 1) * V_DIM)
        q1, q2 = _split_maps(q_ref[:, sl])
        k = k_ref[0, 0, :, sl].astype(BF)
        v = v_ref[0, 0, :, sl].astype(BF)
        p1 = jax.nn.softmax(_dot_nt(q1, k), axis=-1)
        p2 = jax.nn.softmax(_dot_nt(q2, k), axis=-1)
        o = _dot((p1 - lam * p2).astype(BF), v)
        o_ref[:, sl] = _sub_norm(o, g, lam_init).astype(BF)


def _attn_prompt(q, state_k, state_v, lam_params, subln_g, layer):
    m_rows = q.shape[0]
    lam_init = 0.8 - 0.6 * math.exp(-0.3 * layer)
    st = pl.BlockSpec((1, 1, SEQ, D_MODEL), lambda b: (b, layer, 0, 0))
    return pl.pallas_call(
        functools.partial(_attn_prompt_kernel, lam_init=lam_init),
        grid=(m_rows // SEQ,),
        in_specs=[pl.BlockSpec((SEQ, QK_WIDTH), lambda b: (b, 0)), st, st,
                  pl.BlockSpec((1, 4, QK_DIM), lambda b: (layer, 0, 0)),
                  pl.BlockSpec((1, 1, V_DIM), lambda b: (layer, 0, 0))],
        out_specs=pl.BlockSpec((SEQ, V_WIDTH), lambda b: (b, 0)),
        out_shape=jax.ShapeDtypeStruct((m_rows, V_WIDTH), BF),
        compiler_params=_params(1, 32),
        name="attn_prompt",
    )(q, state_k, state_v, lam_params, subln_g.reshape(DEPTH, 1, V_DIM))


def _attn_sample_kernel(q_ref, kn_ref, kc_ref, vn_ref, vc_ref, lam_ref, g_ref, o_ref,
                        qz_scr, vt_new, vt_old, s_new0, s_new1, s_old0, s_old1, m_scr0, m_scr1,
                        *, lam_init, steps_per_head):
    s_new, s_old, m_scr = (s_new0, s_new1), (s_old0, s_old1), (m_scr0, m_scr1)
    g = pl.program_id(0)

    @pl.when(g == 0)
    def _():
        for ref in (s_new1, s_old1, m_scr1):
            ref[...] = jnp.zeros_like(ref)

    @pl.when(jnp.maximum(g - 1, 0) % steps_per_head == 0)
    def _():
        for c in range(DEC_SEQ // ATT_CK):
            vt_new[c] = vn_ref[c * ATT_CK:(c + 1) * ATT_CK, :].astype(F32).T.astype(BF)
        vt_old[...] = vc_ref[0, 0].T.astype(BF)

    def step(cur):
        prv = 1 - cur
        q1, q2 = _split_maps(q_ref[...])
        qz_scr[0] = q1
        qz_scr[1] = q2

        def stages(k_c, vt_c, s_dst, s_src, stats):
            mx, l, acc = stats
            new_mx, new_l, new_acc = [], [], []
            n_keys = k_c.shape[0]
            for mp in range(2):
                s = _dot_nt(k_c, qz_scr[mp])
                s_dst[mp] = s
                run = mx[mp]
                for r in range(n_keys // 8):
                    run = jnp.maximum(run, s[r * 8:(r + 1) * 8, :])
                new_mx.append(run)
            for mp in range(2):
                m8 = m_scr[prv][mp]
                run = l[mp]
                strips = []
                for r in range(n_keys // 16):
                    p_lo = jnp.exp2(s_src[mp, r * 16:r * 16 + 8, :] - m8)
                    p_hi = jnp.exp2(s_src[mp, r * 16 + 8:r * 16 + 16, :] - m8)
                    run = run + p_lo + p_hi
                    strips.append(jnp.concatenate([p_lo, p_hi], axis=0).astype(BF))
                new_l.append(run)
                new_acc.append(acc[mp] + _dot(vt_c, jnp.concatenate(strips, axis=0)))
            return tuple(new_mx), tuple(new_l), tuple(new_acc)

        neg = jnp.full((8, ATT_Q), -jnp.inf, F32)
        zero = jnp.zeros((8, ATT_Q), F32)
        zero_acc = jnp.zeros((V_DIM, ATT_Q), F32)
        stats = stages(kc_ref[0, 0].astype(BF), vt_old[...], s_old[cur], s_old[prv],
                       ((neg, neg), (zero, zero), (zero_acc, zero_acc)))
        for c in range(DEC_SEQ // ATT_CK):
            stats = stages(kn_ref[c * ATT_CK:(c + 1) * ATT_CK, :], vt_new[c], s_new[cur].at[c],
                           s_new[prv].at[c], stats)
        mx, l, acc = stats

        for mp in range(2):
            m_scr[cur][mp] = jnp.broadcast_to(jnp.max(mx[mp], axis=0, keepdims=True), (8, ATT_Q))
        l1 = jnp.sum(l[0], axis=0, keepdims=True)
        l2 = jnp.sum(l[1], axis=0, keepdims=True)
        ot = acc[0] - (_lam(lam_ref, lam_init) * l1 / l2) * acc[1]
        ot = ot * lax.rsqrt(jnp.mean(ot * ot, axis=0, keepdims=True) + EPS * l1 * l1)
        o_ref[...] = (ot.T * g_ref[0] * (1.0 - lam_init)).astype(BF)

    pl.when(g % 2 == 0)(lambda: step(0))
    pl.when(g % 2 == 1)(lambda: step(1))


def _attn_sample(q, k, v, cache_k, cache_v, lam_params, subln_g, layer):
    m_rows = q.shape[0]
    nq = DEC_SEQ // ATT_Q
    n_blocks = (m_rows // DEC_SEQ) * N_HEADS * nq
    nch = DEC_SEQ // ATT_CK
    lam_init = 0.8 - 0.6 * math.exp(-0.3 * layer)

    def decode(g, lag):
        n = jnp.clip(g - lag, 0, n_blocks - 1)
        return n // (N_HEADS * nq), (n // nq) % N_HEADS, n % nq

    def blk(lag):
        def index(g):
            b, h, i = decode(g, lag)
            return b * nq + i, h
        return pl.BlockSpec((ATT_Q, V_DIM), index)

    def new(lag):
        return pl.BlockSpec((DEC_SEQ, V_DIM), lambda g: decode(g, lag)[:2])

    def old(lag):
        def index(g):
            b, h, _ = decode(g, lag)
            return b, layer, 0, h
        return pl.BlockSpec((1, 1, PAST_LEN, V_DIM), index)

    per_slot = [pltpu.VMEM((nch, 2, ATT_CK, ATT_Q), F32), pltpu.VMEM((2, PAST_LEN, ATT_Q), F32),
                pltpu.VMEM((2, 8, ATT_Q), F32)]
    return pl.pallas_call(
        functools.partial(_attn_sample_kernel, lam_init=lam_init, steps_per_head=nq),
        grid=(n_blocks + 1,),
        in_specs=[blk(0), new(0), old(0), new(1), old(1),
                  pl.BlockSpec((1, 4, QK_DIM), lambda g: (layer, 0, 0)),
                  pl.BlockSpec((1, 1, V_DIM), lambda g: (layer, 0, 0))],
        out_specs=blk(1),
        out_shape=jax.ShapeDtypeStruct((m_rows, V_WIDTH), BF),
        scratch_shapes=[pltpu.VMEM((2, ATT_Q, V_DIM), BF), pltpu.VMEM((nch, V_DIM, ATT_CK), BF),
                        pltpu.VMEM((V_DIM, PAST_LEN), BF)]
                       + [shape for shape in per_slot for _ in range(2)],
        compiler_params=_params(1, 56),
        name="attn_sample",
    )(q, k, cache_k, v, cache_v, lam_params, subln_g.reshape(DEPTH, 1, V_DIM))


def _mix_out_kernel(hn_ref, fr_ref, o_ref, x_ref, wg_ref, wf_ref, wa_ref, wo_ref, mods_ref, g2_ref,
                    xo_ref, hno_ref):
    hn = hn_ref[...]
    a_four = _dot(fr_ref[...], wf_ref[...])
    mixed = jax.nn.sigmoid(_dot(hn, wg_ref[:, :D_MODEL])) * a_four
    a_attn = _dot(o_ref[...], wa_ref[...])
    mixed += jax.nn.sigmoid(_dot(hn, wg_ref[:, D_MODEL:])) * a_attn
    mix = _dot(mixed.astype(BF), wo_ref[...])
    m = mods_ref[0]
    x = x_ref[...] + m[2:3] * mix
    xo_ref[...] = x
    hno_ref[...] = _norm_mod(x, g2_ref[...], m[4:5], m[3:4]).astype(BF)


def _mix_out(hn, fr, o, x, w_gate, w_four, w_attn, w_o, mods, norm2_g, seq_len):
    m_rows = hn.shape[0]
    row = lambda w: pl.BlockSpec((TM, w), lambda i: (i, 0))
    return pl.pallas_call(
        _mix_out_kernel,
        grid=(m_rows // TM,),
        in_specs=[row(D_MODEL), row(FOURIER_WIDTH), row(V_WIDTH), row(D_MODEL),
                  _resident(w_gate.shape), _resident(w_four.shape), _resident(w_attn.shape),
                  _resident(w_o.shape), _mods_spec(mods.shape[0], seq_len), _resident((1, D_MODEL))],
        out_specs=[row(D_MODEL), row(D_MODEL)],
        out_shape=[jax.ShapeDtypeStruct((m_rows, D_MODEL), F32),
                   jax.ShapeDtypeStruct((m_rows, D_MODEL), BF)],
        compiler_params=_params(1, 56),
        name="mix_out",
    )(hn, fr, o, x, w_gate, w_four, w_attn, w_o, mods, norm2_g)


def _ffn_kernel(hp_ref, hn_ref, hx_ref, x_ref, wup_ref, cw_ref, cb_ref, wdn_ref, mods_ref, gn_ref,
                modsn_ref, *rest, seq_len, last):
    if last:
        y_ref, lhs_scr, u_scr, h_scr = rest
    else:
        xo_ref, hno_ref, lhs_scr, u_scr, h_scr = rest
    lhs_scr[0:HALO] = hp_ref[...]
    lhs_scr[HALO:HALO + TM] = hn_ref[...]
    lhs_scr[HALO + TM:] = hx_ref[...]
    lhs = lhs_scr[...]
    pos = (pl.program_id(0) * TM + lax.broadcasted_iota(jnp.int32, (TM, 1), 0)) % seq_len
    has_prev = pos != 0
    has_next = pos != seq_len - 1

    def conv(col, slot):
        u_scr[slot] = _dot(lhs, wup_ref[:, col:col + FF_CHUNK])
        prev = jnp.where(has_prev, u_scr[slot, pl.ds(HALO - 1, TM), :], 0.0)
        cur = u_scr[slot, pl.ds(HALO, TM), :]
        nxt = jnp.where(has_next, u_scr[slot, pl.ds(HALO + 1, TM), :], 0.0)
        w = cw_ref[:, col:col + FF_CHUNK]
        return prev * w[0:1] + cur * w[1:2] + nxt * w[2:3] + cb_ref[:, col:col + FF_CHUNK]

    for c in range(D_FF // FF_CHUNK):
        val = conv(c * FF_CHUNK, 0)
        gate = conv(D_FF + c * FF_CHUNK, 1)
        h_scr[:, c * FF_CHUNK:(c + 1) * FF_CHUNK] = (gate * jax.nn.sigmoid(gate) * val).astype(BF)

    m = mods_ref[0]
    x = x_ref[...] + m[5:6] * _dot(h_scr[...], wdn_ref[...])
    if last:
        y_ref[...] = x * lax.rsqrt(jnp.mean(x * x, axis=-1, keepdims=True) + EPS) * gn_ref[...]
    else:
        xo_ref[...] = x
        mn = modsn_ref[0]
        hno_ref[...] = _norm_mod(x, gn_ref[...], mn[1:2], mn[0:1]).astype(BF)


def _ffn(hn, x, w_up, conv_w, conv_b, w_down, mods, g_next, mods_next, seq_len, last):
    m_rows = hn.shape[0]
    per = TM // HALO
    row = lambda w: pl.BlockSpec((TM, w), lambda i: (i, 0))
    prev = pl.BlockSpec((HALO, D_MODEL), lambda i: (jnp.maximum(i * per - 1, 0), 0))
    nxt = pl.BlockSpec((HALO, D_MODEL), lambda i: (jnp.minimum((i + 1) * per, m_rows // HALO - 1), 0))
    if last:
        out_specs = row(D_MODEL)
        out_shape = jax.ShapeDtypeStruct((m_rows, D_MODEL), F32)
    else:
        out_specs = [row(D_MODEL), row(D_MODEL)]
        out_shape = [jax.ShapeDtypeStruct((m_rows, D_MODEL), F32),
                     jax.ShapeDtypeStruct((m_rows, D_MODEL), BF)]
    return pl.pallas_call(
        functools.partial(_ffn_kernel, seq_len=seq_len, last=last),
        grid=(m_rows // TM,),
        in_specs=[prev, row(D_MODEL), nxt, row(D_MODEL),
                  _resident(w_up.shape), _resident(conv_w.shape), _resident(conv_b.shape),
                  _resident(w_down.shape), _mods_spec(mods.shape[0], seq_len),
                  _resident((1, D_MODEL)), _mods_spec(mods_next.shape[0], seq_len)],
        out_specs=out_specs,
        out_shape=out_shape,
        scratch_shapes=[pltpu.VMEM((TM + 2 * HALO, D_MODEL), BF),
                        pltpu.VMEM((2, TM + 2 * HALO, FF_CHUNK), F32),
                        pltpu.VMEM((TM, D_FF), BF)],
        compiler_params=_params(1, 56),
        name="ffn",
    )(hn, hn, hn, x, w_up, conv_w, conv_b, w_down, mods, g_next, mods_next)


def _cos_sin(num, den):
    ang = (2.0 * math.pi / den) * (num % den).astype(F32)
    return jnp.cos(ang), jnp.sin(ang)


def _dft_tables():
    n = jnp.arange(FOURIER_GROUP_DIM)
    c, s = _cos_sin(n[:, None] * n[None, :], FOURIER_GROUP_DIM)
    cs = (jnp.concatenate([c, -s], axis=1) * FOURIER_GROUP_DIM ** -0.5).astype(BF)
    t = jnp.arange(SEQ)
    c, s = _cos_sin(t[:, None] * t[None, :], SEQ)
    ct, st = (c * SEQ ** -0.5).astype(BF), (s * SEQ ** -0.5).astype(BF)
    r = jnp.arange(N_SIDE)
    c, s = _cos_sin(r[:, None] * r[None, :], N_SIDE)
    ff = (jnp.concatenate([c, s], axis=0) * N_SIDE ** -0.5).astype(BF)
    k = r[:, None, None] + N_SIDE * r[None, :, None]
    c, s = _cos_sin(k * r[None, None, :], N_SIDE * N_SIDE)
    gc, gs = (c * N_SIDE ** -0.5).astype(BF), (s * N_SIDE ** -0.5).astype(BF)
    return cs, ct, st, ff, gc, gs


def _rope_tables():
    half = QK_DIM // 2
    inv_freq = 1.0 / (ROPE_BASE ** (jnp.arange(0, half, 2, dtype=F32) / half))
    t = jnp.arange(DEC_SEQ)
    row = (t // GRID_W).astype(F32)
    col = (t % GRID_W).astype(F32)
    lane = jnp.arange(128)
    d = lane % QK_DIM
    freq = inv_freq[d % (half // 2)]
    ang = jnp.where((d < half)[None, :], row[:, None] * freq[None, :], col[:, None] * freq[None, :])
    cos, sin = jnp.cos(ang), jnp.sin(ang)
    upper = ((d % half) >= half // 2)[None, :]
    sin_hi = jnp.where(upper, sin, 0.0)
    sin_lo = jnp.where(upper, 0.0, -sin)
    return cos, sin_lo, sin_hi


def _trunk(x, mods, w, seq_len, layer_fns):
    in_proj, fourier, attention = layer_fns
    hn = _prenorm(x, w["norm1_g"][0:1], mods[0], seq_len)
    for l in range(DEPTH):
        y, q, kv = in_proj(hn, l)
        fr = fourier(y)
        o = attention(q, kv, l)
        x, hn2 = _mix_out(hn, fr, o, x, w["w_gate"][l], w["w_fourier"][l], w["w_attn"][l], w["w_o"][l],
                          mods[l], w["norm2_g"][l:l + 1], seq_len)
        last = l == DEPTH - 1
        g_next = w["final_g"] if last else w["norm1_g"][l + 1:l + 2]
        res = _ffn(hn2, x, w["w_up"][l], w["conv_w"][l], w["conv_b"][l:l + 1], w["w_down"][l],
                   mods[l], g_next, mods[l if last else l + 1], seq_len, last)
        if last:
            return res
        x, hn = res


def kernel(x_prompt, x_sample, c, cache_k, cache_v, c_ctx, norm1_g, norm2_g, final_g, w_ada, b_ada, w_in,
           w_fourier, lam_params, subln_g, w_attn, w_o, w_up, conv_w, conv_b, w_down):
    w_in_b = w_in.astype(BF)
    w = {
        "norm1_g": norm1_g, "norm2_g": norm2_g, "final_g": final_g.reshape(1, D_MODEL),
        "w_fqkv": w_in_b[:, :, :FQKV_WIDTH], "w_gate": w_in_b[:, :, FQKV_WIDTH:],
        "w_fourier": w_fourier.astype(BF), "w_attn": w_attn.astype(BF), "w_o": w_o.astype(BF),
        "w_up": w_up.astype(BF), "conv_w": conv_w, "conv_b": conv_b, "w_down": w_down.astype(BF),
    }
    cond = jnp.concatenate([c_ctx[None, :], c, jnp.zeros((8 - 1 - DEC_BATCH, D_MODEL), F32)], axis=0)
    mods = _adaln(cond, w_ada, b_ada)[:, :1 + DEC_BATCH].reshape(DEPTH, 1 + DEC_BATCH, N_MOD, D_MODEL)
    cs, ct, st, ff, gc, gs = _dft_tables()
    rope_tabs = _rope_tables()

    state = {"k": jnp.zeros((BATCH, DEPTH, SEQ, QK_WIDTH), F32),
             "v": jnp.zeros((BATCH, DEPTH, SEQ, V_WIDTH), F32)}

    def in_proj_p(hn, l):
        y, q, state["k"], state["v"] = _in_proj_prompt(hn, w["w_fqkv"][l], cs, state["k"], state["v"], l)
        return y, q, None

    y_prompt = _trunk(
        x_prompt.reshape(BATCH * SEQ, D_MODEL), mods[:, 0:1], w, SEQ,
        (in_proj_p,
         lambda y: _fourier_prompt(y, ct, st),
         lambda q, kv, l: _attn_prompt(q, state["k"], state["v"], lam_params, subln_g, l)))

    ck = cache_k.reshape(DEC_BATCH, DEPTH, PAST_LEN, QK_WIDTH)
    cv = cache_v.reshape(DEC_BATCH, DEPTH, PAST_LEN, V_WIDTH)

    def in_proj_s(hn, l):
        y, q, k, v = _in_proj_sample(hn, w["w_fqkv"][l], cs, rope_tabs)
        return y, q, (k, v)

    y_sample = _trunk(
        x_sample.reshape(DEC_BATCH * DEC_SEQ, D_MODEL), mods[:, 1:], w, DEC_SEQ,
        (in_proj_s,
         lambda y: _fourier_sample(y, ff, gc, gs),
         lambda q, kv, l: _attn_sample(q, kv[0], kv[1], ck, cv, lam_params, subln_g, l)))

    return (y_prompt.reshape(BATCH, SEQ, D_MODEL),
            y_sample.reshape(DEC_BATCH, DEC_SEQ, D_MODEL),
            state["k"].reshape(BATCH, DEPTH, SEQ, N_HEADS, 2, QK_DIM),
            state["v"].reshape(BATCH, DEPTH, SEQ, N_HEADS, V_DIM))
```

```python
import functools
import math

import jax
import jax.numpy as jnp
from jax import lax
from jax.experimental import pallas as pl
from jax.experimental.pallas import tpu as pltpu

D_MODEL = 1024
BATCH = 32
SEQ = 256
DEPTH = 4
DEC_BATCH = 2
DEC_SEQ = 4096
PAST_LEN = 256
GRID_W = 64
QK_DIM = 64
V_DIM = 2 * QK_DIM
N_HEADS = D_MODEL // (2 * QK_DIM)
FOURIER_GROUPS = 4
FOURIER_WIDTH = D_MODEL // 2
FOURIER_GROUP_DIM = FOURIER_WIDTH // FOURIER_GROUPS
QK_WIDTH = N_HEADS * 2 * QK_DIM
V_WIDTH = N_HEADS * V_DIM
GATE_WIDTH = 2 * D_MODEL
FQKV_WIDTH = FOURIER_WIDTH + 2 * QK_WIDTH + V_WIDTH
D_FF = ((8 * D_MODEL // 3 + 127) // 128) * 128
ROPE_BASE = 10000.0
EPS = 1e-6
N_MOD = 6
SCALE = QK_DIM ** -0.5
LOG2E = math.log2(math.e)

BF = jnp.bfloat16
F32 = jnp.float32

MIB = 1024 * 1024
TM = 512
HALO = 16
FF_CHUNK = 256
ATT_Q = 256
ATT_CK = 512
N_SIDE = 64
S1_COLS = 4
S2_ROWS = 8


def _params(n_axes, vmem_mib, flags=None):
    return pltpu.CompilerParams(dimension_semantics=("arbitrary",) * n_axes,
                                vmem_limit_bytes=vmem_mib * MIB, flags=flags)


def _resident(shape):
    zeros = (0,) * len(shape)
    return pl.BlockSpec(shape, lambda *_: zeros, pipeline_mode=pl.Buffered(1))


def _layer_resident(stacked, layer):
    tail = stacked.shape[1:]
    zeros = (0,) * len(tail)
    return pl.BlockSpec((None,) + tail, lambda *_: (layer,) + zeros, pipeline_mode=pl.Buffered(1))


def _dot(a, b):
    return jnp.dot(a, b, preferred_element_type=F32)


def _dot_nt(a, b):
    return lax.dot_general(a, b, (((1,), (1,)), ((), ())), preferred_element_type=F32)


def _norm_mod(x, g, sc, sh):
    y = x * lax.rsqrt(jnp.mean(x * x, axis=-1, keepdims=True) + EPS)
    return (y * g) * (1.0 + sc) + sh


def _adaln_kernel(c_ref, w_ref, b_ref, o_ref):
    c = c_ref[...]
    s = (c * jax.nn.sigmoid(c)).astype(BF)
    o_ref[0] = _dot(s, w_ref[0].astype(BF)) + b_ref[0]


def _adaln(cond, w_ada, b_ada):
    rows = cond.shape[0]
    return pl.pallas_call(
        _adaln_kernel,
        grid=(DEPTH, N_MOD),
        in_specs=[pl.BlockSpec((rows, D_MODEL), lambda l, j: (0, 0)),
                  pl.BlockSpec((1, D_MODEL, D_MODEL), lambda l, j: (l, 0, j)),
                  pl.BlockSpec((1, 1, D_MODEL), lambda l, j: (l, 0, j))],
        out_specs=pl.BlockSpec((1, rows, D_MODEL), lambda l, j: (l, 0, j)),
        out_shape=jax.ShapeDtypeStruct((DEPTH, rows, N_MOD * D_MODEL), F32),
        compiler_params=_params(2, 32),
        name="adaln",
    )(cond, w_ada, b_ada.reshape(DEPTH, 1, N_MOD * D_MODEL))


def _prenorm_kernel(x_ref, g_ref, mods_ref, hn_ref):
    m = mods_ref[0]
    hn_ref[...] = _norm_mod(x_ref[...], g_ref[...], m[1:2], m[0:1]).astype(BF)


def _mods_spec(n_rows, seq_len):
    if n_rows == 1:
        return pl.BlockSpec((1, N_MOD, D_MODEL), lambda i: (0, 0, 0))
    return pl.BlockSpec((1, N_MOD, D_MODEL), lambda i: ((i * TM) // seq_len, 0, 0))


def _prenorm(x, g, mods, seq_len):
    m_rows = x.shape[0]
    row = pl.BlockSpec((TM, D_MODEL), lambda i: (i, 0))
    return pl.pallas_call(
        _prenorm_kernel,
        grid=(m_rows // TM,),
        in_specs=[row, _resident((1, D_MODEL)), _mods_spec(mods.shape[0], seq_len)],
        out_specs=row,
        out_shape=jax.ShapeDtypeStruct((m_rows, D_MODEL), BF),
        compiler_params=_params(1, 32),
        name="prenorm",
    )(x, g, mods)


def _channel_dft(f, cs_ref, y_ref):
    gd = FOURIER_GROUP_DIM
    for g in range(FOURIER_GROUPS):
        yg = _dot(f[:, g * gd:(g + 1) * gd], cs_ref[...])
        y_ref[:, g * gd:(g + 1) * gd] = yg[:, :gd].astype(BF)
        y_ref[:, FOURIER_WIDTH + g * gd:FOURIER_WIDTH + (g + 1) * gd] = yg[:, gd:].astype(BF)


def _rope(x, cos, sin_lo, sin_hi):
    cols = []
    for j in range(x.shape[1] // 128):
        xb = x[:, j * 128:(j + 1) * 128]
        cols.append(xb * cos + pltpu.roll(xb, 16, 1) * sin_hi + pltpu.roll(xb, 112, 1) * sin_lo)
    return jnp.concatenate(cols, axis=1)


def _in_proj_prompt_kernel(hn_ref, w_ref, cs_ref, sk_any, sv_any, y_ref, q_ref, sk_ref, sv_ref):
    del sk_any, sv_any
    hn = hn_ref[...]
    o = FOURIER_WIDTH
    f = _dot(hn, w_ref[:, 0:o]).astype(BF)
    _channel_dft(f, cs_ref, y_ref)
    q_ref[...] = (_dot(hn, w_ref[:, o:o + QK_WIDTH]) * (SCALE * LOG2E)).astype(BF)
    o += QK_WIDTH
    sk_ref[...] = _dot(hn, w_ref[:, o:o + QK_WIDTH]).reshape(sk_ref.shape)
    o += QK_WIDTH
    sv_ref[...] = _dot(hn, w_ref[:, o:o + V_WIDTH]).reshape(sv_ref.shape)


def _in_proj_prompt(hn, w_fqkv, cs, state_k, state_v, layer):
    m_rows = hn.shape[0]
    nb = TM // SEQ
    row = lambda w: pl.BlockSpec((TM, w), lambda i: (i, 0))
    st = pl.BlockSpec((nb, 1, SEQ, D_MODEL), lambda i: (i, layer, 0, 0))
    any_spec = pl.BlockSpec(memory_space=pl.ANY)
    return pl.pallas_call(
        _in_proj_prompt_kernel,
        grid=(m_rows // TM,),
        in_specs=[row(D_MODEL), _layer_resident(w_fqkv, layer), _resident(cs.shape), any_spec, any_spec],
        out_specs=[row(D_MODEL), row(QK_WIDTH), st, st],
        out_shape=[jax.ShapeDtypeStruct((m_rows, D_MODEL), BF),
                   jax.ShapeDtypeStruct((m_rows, QK_WIDTH), BF),
                   jax.ShapeDtypeStruct(state_k.shape, F32),
                   jax.ShapeDtypeStruct(state_v.shape, F32)],
        input_output_aliases={3: 2, 4: 3},
        compiler_params=_params(1, 48),
        name="in_proj_prompt",
    )(hn, w_fqkv, cs, state_k, state_v)


def _in_proj_sample_kernel(hn_ref, w_ref, cs_ref, cos_ref, slo_ref, shi_ref, y_ref, q_ref, k_ref, v_ref):
    hn = hn_ref[...]
    o = FOURIER_WIDTH
    f = _dot(hn, w_ref[:, 0:o]).astype(BF)
    _channel_dft(f, cs_ref, y_ref)
    cos, slo, shi = cos_ref[...], slo_ref[...], shi_ref[...]
    q = _rope(_dot(hn, w_ref[:, o:o + QK_WIDTH]), cos, slo, shi)
    q_ref[...] = (q * (SCALE * LOG2E)).astype(BF)
    o += QK_WIDTH
    k_ref[...] = _rope(_dot(hn, w_ref[:, o:o + QK_WIDTH]), cos, slo, shi).astype(BF)
    o += QK_WIDTH
    v_ref[...] = _dot(hn, w_ref[:, o:o + V_WIDTH]).astype(BF)


def _in_proj_sample(hn, w_fqkv, cs, rope_tabs, layer):
    m_rows = hn.shape[0]
    row = lambda w: pl.BlockSpec((TM, w), lambda i: (i, 0))
    tab = pl.BlockSpec((TM, 128), lambda i: (i % (DEC_SEQ // TM), 0))
    return pl.pallas_call(
        _in_proj_sample_kernel,
        grid=(m_rows // TM,),
        in_specs=[row(D_MODEL), _layer_resident(w_fqkv, layer), _resident(cs.shape), tab, tab, tab],
        out_specs=[row(D_MODEL), row(QK_WIDTH), row(QK_WIDTH), row(V_WIDTH)],
        out_shape=[jax.ShapeDtypeStruct((m_rows, D_MODEL), BF),
                   jax.ShapeDtypeStruct((m_rows, QK_WIDTH), BF),
                   jax.ShapeDtypeStruct((m_rows, QK_WIDTH), BF),
                   jax.ShapeDtypeStruct((m_rows, V_WIDTH), BF)],
        compiler_params=_params(1, 48),
        name="in_proj_sample",
    )(hn, w_fqkv, cs, *rope_tabs)


def _fourier_prompt_kernel(y_ref, ct_ref, st_ref, fr_ref):
    y = y_ref[...]
    fr = _dot(ct_ref[...], y[:, :FOURIER_WIDTH]) + _dot(st_ref[...], y[:, FOURIER_WIDTH:])
    fr_ref[...] = fr.astype(BF)


def _fourier_prompt(y, ct, st):
    m_rows = y.shape[0]
    return pl.pallas_call(
        _fourier_prompt_kernel,
        grid=(m_rows // SEQ,),
        in_specs=[pl.BlockSpec((SEQ, D_MODEL), lambda i: (i, 0)), _resident(ct.shape), _resident(st.shape)],
        out_specs=pl.BlockSpec((SEQ, FOURIER_WIDTH), lambda i: (i, 0)),
        out_shape=jax.ShapeDtypeStruct((m_rows, FOURIER_WIDTH), BF),
        compiler_params=_params(1, 32),
        name="fourier_prompt",
    )(y, ct, st)


def _fourier_stage1_kernel(y_ref, ff_ref, a_ref):
    pq = _dot(ff_ref[...], y_ref[0])
    p, q = pq[:N_SIDE], pq[N_SIDE:]
    w = FOURIER_WIDTH
    for s in range(S1_COLS):
        c = s * 2 * w
        a_ref[0, :, c:c + w] = (p[:, c:c + w] + q[:, c + w:c + 2 * w]).astype(BF)
        a_ref[0, :, c + w:c + 2 * w] = (p[:, c + w:c + 2 * w] - q[:, c:c + w]).astype(BF)


def _fourier_stage2_kernel(a_ref, gc_ref, gs_ref, fr_ref):
    w = FOURIER_WIDTH
    for kk in range(S2_ROWS):
        a = a_ref[0, kk * N_SIDE:(kk + 1) * N_SIDE, :]
        xr = _dot(gc_ref[kk], a[:, :w]) + _dot(gs_ref[kk], a[:, w:])
        fr_ref[0, :, kk * w:(kk + 1) * w] = xr.astype(BF)


def _fourier_sample(y, ff, gc, gs):
    nb = y.shape[0] // DEC_SEQ
    row_len = N_SIDE * D_MODEL
    y3 = y.reshape(nb, N_SIDE, row_len)
    cols = S1_COLS * D_MODEL
    a = pl.pallas_call(
        _fourier_stage1_kernel,
        grid=(nb, row_len // cols),
        in_specs=[pl.BlockSpec((1, N_SIDE, cols), lambda b, j: (b, 0, j)),
                  pl.BlockSpec(ff.shape, lambda b, j: (0, 0))],
        out_specs=pl.BlockSpec((1, N_SIDE, cols), lambda b, j: (b, 0, j)),
        out_shape=jax.ShapeDtypeStruct((nb, N_SIDE, row_len), BF),
        compiler_params=_params(2, 32),
        name="fourier_stage1",
    )(y3, ff)
    a = a.reshape(nb, N_SIDE * N_SIDE, D_MODEL)
    fr = pl.pallas_call(
        _fourier_stage2_kernel,
        grid=(nb, N_SIDE // S2_ROWS),
        in_specs=[pl.BlockSpec((1, S2_ROWS * N_SIDE, D_MODEL), lambda b, j: (b, j, 0)),
                  pl.BlockSpec((S2_ROWS, N_SIDE, N_SIDE), lambda b, j: (j, 0, 0)),
                  pl.BlockSpec((S2_ROWS, N_SIDE, N_SIDE), lambda b, j: (j, 0, 0))],
        out_specs=pl.BlockSpec((1, N_SIDE, S2_ROWS * FOURIER_WIDTH), lambda b, j: (b, 0, j)),
        out_shape=jax.ShapeDtypeStruct((nb, N_SIDE, N_SIDE * FOURIER_WIDTH), BF),
        compiler_params=_params(2, 32),
        name="fourier_stage2",
    )(a, gc, gs)
    return fr.reshape(nb * DEC_SEQ, FOURIER_WIDTH)


def _lam(lam_ref, lam_init):
    lp = lam_ref[0]
    a = jnp.sum(lp[0:1] * lp[1:2], axis=(0, 1), keepdims=True)
    b = jnp.sum(lp[2:3] * lp[3:4], axis=(0, 1), keepdims=True)
    return jnp.exp(a) - jnp.exp(b) + lam_init


def _split_maps(q):
    lane = lax.broadcasted_iota(jnp.int32, q.shape, 1)
    zero = jnp.zeros_like(q)
    return jnp.where(lane < QK_DIM, q, zero), jnp.where(lane >= QK_DIM, q, zero)


def _attn_prompt_kernel(q_ref, k_ref, v_ref, lam_ref, g_ref, o_ref, s_scr0, s_scr1, *, lam_init):
    lam = _lam(lam_ref, lam_init)
    s_scr = (s_scr0, s_scr1)

    def scores(h):
        sl = slice(h * V_DIM, (h + 1) * V_DIM)
        k = k_ref[0, 0, :, sl].astype(BF)
        maxes = []
        for mp, qz in enumerate(_split_maps(q_ref[:, sl])):
            s = _dot_nt(k, qz)
            s_scr[h % 2][mp] = s
            run = s[0:8, :]
            for r in range(1, SEQ // 8):
                run = jnp.maximum(run, s[r * 8:(r + 1) * 8, :])
            maxes.append(jnp.broadcast_to(jnp.max(run, axis=0, keepdims=True), (8, SEQ)))
        return maxes

    def values(h, maxes):
        sl = slice(h * V_DIM, (h + 1) * V_DIM)
        vt = v_ref[0, 0, :, sl].T.astype(BF)
        acc, l = [], []
        for mp in range(2):
            run = jnp.zeros((8, SEQ), F32)
            strips = []
            for r in range(SEQ // 16):
                p_lo = jnp.exp2(s_scr[h % 2][mp, r * 16:r * 16 + 8, :] - maxes[mp])
                p_hi = jnp.exp2(s_scr[h % 2][mp, r * 16 + 8:r * 16 + 16, :] - maxes[mp])
                run = run + p_lo + p_hi
                strips.append(jnp.concatenate([p_lo, p_hi], axis=0).astype(BF))
            l.append(jnp.sum(run, axis=0, keepdims=True))
            acc.append(_dot(vt, jnp.concatenate(strips, axis=0)))
        ot = acc[0] - (lam * l[0] / l[1]) * acc[1]
        ot = ot * lax.rsqrt(jnp.mean(ot * ot, axis=0, keepdims=True) + EPS * l[0] * l[0])
        o_ref[:, sl] = (ot.T * g_ref[0] * (1.0 - lam_init)).astype(BF)

    maxes = scores(0)
    for h in range(N_HEADS):
        following = scores(h + 1) if h + 1 < N_HEADS else None
        values(h, maxes)
        maxes = following


def _attn_prompt(q, state_k, state_v, lam_params, subln_g, layer):
    m_rows = q.shape[0]
    lam_init = 0.8 - 0.6 * math.exp(-0.3 * layer)
    st = pl.BlockSpec((1, 1, SEQ, D_MODEL), lambda b: (b, layer, 0, 0))
    return pl.pallas_call(
        functools.partial(_attn_prompt_kernel, lam_init=lam_init),
        grid=(m_rows // SEQ,),
        in_specs=[pl.BlockSpec((SEQ, QK_WIDTH), lambda b: (b, 0)), st, st,
                  pl.BlockSpec((1, 4, QK_DIM), lambda b: (layer, 0, 0)),
                  pl.BlockSpec((1, 1, V_DIM), lambda b: (layer, 0, 0))],
        out_specs=pl.BlockSpec((SEQ, V_WIDTH), lambda b: (b, 0)),
        out_shape=jax.ShapeDtypeStruct((m_rows, V_WIDTH), BF),
        scratch_shapes=[pltpu.VMEM((2, SEQ, SEQ), F32), pltpu.VMEM((2, SEQ, SEQ), F32)],
        compiler_params=_params(1, 32),
        name="attn_prompt",
    )(q, state_k, state_v, lam_params, subln_g.reshape(DEPTH, 1, V_DIM))


def _attn_sample_kernel(q_ref, kn_ref, kc_ref, vn_ref, vc_ref, lam_ref, g_ref, o_ref,
                        qz_scr, vt_new, vt_old, s_new0, s_new1, s_old0, s_old1, m_scr0, m_scr1,
                        *, lam_init, steps_per_head):
    s_new, s_old, m_scr = (s_new0, s_new1), (s_old0, s_old1), (m_scr0, m_scr1)
    g = pl.program_id(0)

    @pl.when(g == 0)
    def _():
        for ref in (s_new1, s_old1, m_scr1):
            ref[...] = jnp.zeros_like(ref)

    @pl.when(jnp.maximum(g - 1, 0) % steps_per_head == 0)
    def _():
        for c in range(DEC_SEQ // ATT_CK):
            vt_new[c] = vn_ref[c * ATT_CK:(c + 1) * ATT_CK, :].astype(F32).T.astype(BF)
        vt_old[...] = vc_ref[0, 0].T.astype(BF)

    def step(cur):
        prv = 1 - cur
        q1, q2 = _split_maps(q_ref[...])
        qz_scr[0] = q1
        qz_scr[1] = q2

        def stages(k_c, vt_c, s_dst, s_src, stats):
            mx, l, acc = stats
            new_mx, new_l, new_acc = [], [], []
            n_keys = k_c.shape[0]
            for mp in range(2):
                s = _dot_nt(k_c, qz_scr[mp])
                s_dst[mp] = s
                run = mx[mp]
                for r in range(n_keys // 8):
                    run = jnp.maximum(run, s[r * 8:(r + 1) * 8, :])
                new_mx.append(run)
            for mp in range(2):
                m8 = m_scr[prv][mp]
                run = l[mp]
                strips = []
                for r in range(n_keys // 16):
                    p_lo = jnp.exp2(s_src[mp, r * 16:r * 16 + 8, :] - m8)
                    p_hi = jnp.exp2(s_src[mp, r * 16 + 8:r * 16 + 16, :] - m8)
                    run = run + p_lo + p_hi
                    strips.append(jnp.concatenate([p_lo, p_hi], axis=0).astype(BF))
                new_l.append(run)
                new_acc.append(acc[mp] + _dot(vt_c, jnp.concatenate(strips, axis=0)))
            return tuple(new_mx), tuple(new_l), tuple(new_acc)

        neg = jnp.full((8, ATT_Q), -jnp.inf, F32)
        zero = jnp.zeros((8, ATT_Q), F32)
        zero_acc = jnp.zeros((V_DIM, ATT_Q), F32)
        stats = stages(kc_ref[0, 0].astype(BF), vt_old[...], s_old[cur], s_old[prv],
                       ((neg, neg), (zero, zero), (zero_acc, zero_acc)))
        for c in range(DEC_SEQ // ATT_CK):
            stats = stages(kn_ref[c * ATT_CK:(c + 1) * ATT_CK, :], vt_new[c], s_new[cur].at[c],
                           s_new[prv].at[c], stats)
        mx, l, acc = stats

        for mp in range(2):
            m_scr[cur][mp] = jnp.broadcast_to(jnp.max(mx[mp], axis=0, keepdims=True), (8, ATT_Q))
        l1 = jnp.sum(l[0], axis=0, keepdims=True)
        l2 = jnp.sum(l[1], axis=0, keepdims=True)
        ot = acc[0] - (_lam(lam_ref, lam_init) * l1 / l2) * acc[1]
        ot = ot * lax.rsqrt(jnp.mean(ot * ot, axis=0, keepdims=True) + EPS * l1 * l1)
        o_ref[...] = (ot.T * g_ref[0] * (1.0 - lam_init)).astype(BF)

    pl.when(g % 2 == 0)(lambda: step(0))
    pl.when(g % 2 == 1)(lambda: step(1))


def _attn_sample(q, k, v, cache_k, cache_v, lam_params, subln_g, layer):
    m_rows = q.shape[0]
    nq = DEC_SEQ // ATT_Q
    n_blocks = (m_rows // DEC_SEQ) * N_HEADS * nq
    nch = DEC_SEQ // ATT_CK
    lam_init = 0.8 - 0.6 * math.exp(-0.3 * layer)

    def decode(g, lag):
        n = jnp.clip(g - lag, 0, n_blocks - 1)
        return n // (N_HEADS * nq), (n // nq) % N_HEADS, n % nq

    def blk(lag):
        def index(g):
            b, h, i = decode(g, lag)
            return b * nq + i, h
        return pl.BlockSpec((ATT_Q, V_DIM), index)

    def new(lag):
        return pl.BlockSpec((DEC_SEQ, V_DIM), lambda g: decode(g, lag)[:2])

    def old(lag):
        def index(g):
            b, h, _ = decode(g, lag)
            return b, layer, 0, h
        return pl.BlockSpec((1, 1, PAST_LEN, V_DIM), index)

    per_slot = [pltpu.VMEM((nch, 2, ATT_CK, ATT_Q), F32), pltpu.VMEM((2, PAST_LEN, ATT_Q), F32),
                pltpu.VMEM((2, 8, ATT_Q), F32)]
    return pl.pallas_call(
        functools.partial(_attn_sample_kernel, lam_init=lam_init, steps_per_head=nq),
        grid=(n_blocks + 1,),
        in_specs=[blk(0), new(0), old(0), new(1), old(1),
                  pl.BlockSpec((1, 4, QK_DIM), lambda g: (layer, 0, 0)),
                  pl.BlockSpec((1, 1, V_DIM), lambda g: (layer, 0, 0))],
        out_specs=blk(1),
        out_shape=jax.ShapeDtypeStruct((m_rows, V_WIDTH), BF),
        scratch_shapes=[pltpu.VMEM((2, ATT_Q, V_DIM), BF), pltpu.VMEM((nch, V_DIM, ATT_CK), BF),
                        pltpu.VMEM((V_DIM, PAST_LEN), BF)]
                       + [shape for shape in per_slot for _ in range(2)],
        compiler_params=_params(1, 56),
        name="attn_sample",
    )(q, k, cache_k, v, cache_v, lam_params, subln_g.reshape(DEPTH, 1, V_DIM))


def _mix_out_kernel(hn_ref, fr_ref, o_ref, x_ref, wg_ref, wf_ref, wa_ref, wo_ref, mods_ref, g2_ref,
                    xo_ref, hno_ref):
    hn = hn_ref[...]
    a_four = _dot(fr_ref[...], wf_ref[...])
    mixed = jax.nn.sigmoid(_dot(hn, wg_ref[:, :D_MODEL])) * a_four
    a_attn = _dot(o_ref[...], wa_ref[...])
    mixed += jax.nn.sigmoid(_dot(hn, wg_ref[:, D_MODEL:])) * a_attn
    mix = _dot(mixed.astype(BF), wo_ref[...])
    m = mods_ref[0]
    x = x_ref[...] + m[2:3] * mix
    xo_ref[...] = x
    hno_ref[...] = _norm_mod(x, g2_ref[...], m[4:5], m[3:4]).astype(BF)


def _mix_out(hn, fr, o, x, w_gate, w_four, w_attn, w_o, mods, norm2_g, seq_len, layer):
    m_rows = hn.shape[0]
    row = lambda w: pl.BlockSpec((TM, w), lambda i: (i, 0))
    return pl.pallas_call(
        _mix_out_kernel,
        grid=(m_rows // TM,),
        in_specs=[row(D_MODEL), row(FOURIER_WIDTH), row(V_WIDTH), row(D_MODEL),
                  _layer_resident(w_gate, layer), _layer_resident(w_four, layer),
                  _layer_resident(w_attn, layer), _layer_resident(w_o, layer),
                  _mods_spec(mods.shape[0], seq_len), _resident((1, D_MODEL))],
        out_specs=[row(D_MODEL), row(D_MODEL)],
        out_shape=[jax.ShapeDtypeStruct((m_rows, D_MODEL), F32),
                   jax.ShapeDtypeStruct((m_rows, D_MODEL), BF)],
        compiler_params=_params(1, 56),
        name="mix_out",
    )(hn, fr, o, x, w_gate, w_four, w_attn, w_o, mods, norm2_g)


def _ffn_kernel(hp_ref, hn_ref, hx_ref, x_ref, wup_ref, cw_ref, cb_ref, wdn_ref, mods_ref, gn_ref,
                modsn_ref, *rest, seq_len, last):
    if last:
        y_ref, lhs_scr, u_scr, h_scr = rest
    else:
        xo_ref, hno_ref, lhs_scr, u_scr, h_scr = rest
    i = pl.program_id(0)
    if seq_len % TM == 0:
        tiles = seq_len // TM
        blank = jnp.zeros(hp_ref.shape, hp_ref.dtype)
        hp = jnp.where(i % tiles == 0, blank, hp_ref[...])
        hx = jnp.where(i % tiles == tiles - 1, blank, hx_ref[...])
        edge = lambda rows, shift: rows
    else:
        hp, hx = hp_ref[...], hx_ref[...]
        pos = (i * TM + lax.broadcasted_iota(jnp.int32, (TM, 1), 0)) % seq_len
        keep = {-1: pos != 0, 1: pos != seq_len - 1}
        edge = lambda rows, shift: jnp.where(keep[shift], rows, 0.0)
    lhs_scr[0:HALO] = hp
    lhs_scr[HALO:HALO + TM] = hn_ref[...]
    lhs_scr[HALO + TM:] = hx
    lhs = lhs_scr[...]

    def conv(col, slot):
        u_scr[slot] = _dot(lhs, wup_ref[:, col:col + FF_CHUNK])
        prev = edge(u_scr[slot, pl.ds(HALO - 1, TM), :], -1)
        cur = u_scr[slot, pl.ds(HALO, TM), :]
        nxt = edge(u_scr[slot, pl.ds(HALO + 1, TM), :], 1)
        w = cw_ref[:, col:col + FF_CHUNK]
        return prev * w[0:1] + cur * w[1:2] + nxt * w[2:3] + cb_ref[:, col:col + FF_CHUNK]

    for c in range(D_FF // FF_CHUNK):
        val = conv(c * FF_CHUNK, 0)
        gate = conv(D_FF + c * FF_CHUNK, 1)
        h_scr[:, c * FF_CHUNK:(c + 1) * FF_CHUNK] = (gate * jax.nn.sigmoid(gate) * val).astype(BF)

    m = mods_ref[0]
    x = x_ref[...] + m[5:6] * _dot(h_scr[...], wdn_ref[...])
    if last:
        y_ref[...] = x * lax.rsqrt(jnp.mean(x * x, axis=-1, keepdims=True) + EPS) * gn_ref[...]
    else:
        xo_ref[...] = x
        mn = modsn_ref[0]
        hno_ref[...] = _norm_mod(x, gn_ref[...], mn[1:2], mn[0:1]).astype(BF)


def _ffn(hn, x, w_up, conv_w, conv_b, w_down, mods, g_next, mods_next, seq_len, layer):
    m_rows = hn.shape[0]
    last = layer == DEPTH - 1
    per = TM // HALO
    row = lambda w: pl.BlockSpec((TM, w), lambda i: (i, 0))
    prev = pl.BlockSpec((HALO, D_MODEL), lambda i: (jnp.maximum(i * per - 1, 0), 0))
    nxt = pl.BlockSpec((HALO, D_MODEL), lambda i: (jnp.minimum((i + 1) * per, m_rows // HALO - 1), 0))
    if last:
        out_specs = row(D_MODEL)
        out_shape = jax.ShapeDtypeStruct((m_rows, D_MODEL), F32)
    else:
        out_specs = [row(D_MODEL), row(D_MODEL)]
        out_shape = [jax.ShapeDtypeStruct((m_rows, D_MODEL), F32),
                     jax.ShapeDtypeStruct((m_rows, D_MODEL), BF)]
    return pl.pallas_call(
        functools.partial(_ffn_kernel, seq_len=seq_len, last=last),
        grid=(m_rows // TM,),
        in_specs=[prev, row(D_MODEL), nxt, row(D_MODEL),
                  _layer_resident(w_up, layer), _layer_resident(conv_w, layer),
                  _layer_resident(conv_b, layer), _layer_resident(w_down, layer),
                  _mods_spec(mods.shape[0], seq_len),
                  _resident((1, D_MODEL)), _mods_spec(mods_next.shape[0], seq_len)],
        out_specs=out_specs,
        out_shape=out_shape,
        scratch_shapes=[pltpu.VMEM((TM + 2 * HALO, D_MODEL), BF),
                        pltpu.VMEM((2, TM + 2 * HALO, FF_CHUNK), F32),
                        pltpu.VMEM((TM, D_FF), BF)],
        compiler_params=_params(1, 56),
        name="ffn",
    )(hn, hn, hn, x, w_up, conv_w, conv_b, w_down, mods, g_next, mods_next)


def _cos_sin(num, den):
    ang = (2.0 * math.pi / den) * (num % den).astype(F32)
    return jnp.cos(ang), jnp.sin(ang)


def _dft_tables():
    n = jnp.arange(FOURIER_GROUP_DIM)
    c, s = _cos_sin(n[:, None] * n[None, :], FOURIER_GROUP_DIM)
    cs = (jnp.concatenate([c, -s], axis=1) * FOURIER_GROUP_DIM ** -0.5).astype(BF)
    t = jnp.arange(SEQ)
    c, s = _cos_sin(t[:, None] * t[None, :], SEQ)
    ct, st = (c * SEQ ** -0.5).astype(BF), (s * SEQ ** -0.5).astype(BF)
    r = jnp.arange(N_SIDE)
    c, s = _cos_sin(r[:, None] * r[None, :], N_SIDE)
    ff = (jnp.concatenate([c, s], axis=0) * N_SIDE ** -0.5).astype(BF)
    k = r[:, None, None] + N_SIDE * r[None, :, None]
    c, s = _cos_sin(k * r[None, None, :], N_SIDE * N_SIDE)
    gc, gs = (c * N_SIDE ** -0.5).astype(BF), (s * N_SIDE ** -0.5).astype(BF)
    return cs, ct, st, ff, gc, gs


def _rope_tables():
    half = QK_DIM // 2
    inv_freq = 1.0 / (ROPE_BASE ** (jnp.arange(0, half, 2, dtype=F32) / half))
    t = jnp.arange(DEC_SEQ)
    row = (t // GRID_W).astype(F32)
    col = (t % GRID_W).astype(F32)
    lane = jnp.arange(128)
    d = lane % QK_DIM
    freq = inv_freq[d % (half // 2)]
    ang = jnp.where((d < half)[None, :], row[:, None] * freq[None, :], col[:, None] * freq[None, :])
    cos, sin = jnp.cos(ang), jnp.sin(ang)
    upper = ((d % half) >= half // 2)[None, :]
    sin_hi = jnp.where(upper, sin, 0.0)
    sin_lo = jnp.where(upper, 0.0, -sin)
    return cos, sin_lo, sin_hi


def _trunk(x, mods, w, seq_len, layer_fns):
    in_proj, fourier, attention = layer_fns
    hn = _prenorm(x, w["norm1_g"][0:1], mods[0], seq_len)
    for l in range(DEPTH):
        y, q, kv = in_proj(hn, l)
        fr = fourier(y)
        o = attention(q, kv, l)
        x, hn2 = _mix_out(hn, fr, o, x, w["w_gate"], w["w_fourier"], w["w_attn"], w["w_o"],
                          mods[l], w["norm2_g"][l:l + 1], seq_len, l)
        last = l == DEPTH - 1
        g_next = w["final_g"] if last else w["norm1_g"][l + 1:l + 2]
        res = _ffn(hn2, x, w["w_up"], w["conv_w"], w["conv_b"], w["w_down"],
                   mods[l], g_next, mods[l if last else l + 1], seq_len, l)
        if last:
            return res
        x, hn = res


def kernel(x_prompt, x_sample, c, cache_k, cache_v, c_ctx, norm1_g, norm2_g, final_g, w_ada, b_ada, w_in,
           w_fourier, lam_params, subln_g, w_attn, w_o, w_up, conv_w, conv_b, w_down):
    w = {
        "norm1_g": norm1_g, "norm2_g": norm2_g, "final_g": final_g.reshape(1, D_MODEL),
        "w_fqkv": w_in[:, :, :FQKV_WIDTH].astype(BF), "w_gate": w_in[:, :, FQKV_WIDTH:].astype(BF),
        "w_fourier": w_fourier.astype(BF), "w_attn": w_attn.astype(BF), "w_o": w_o.astype(BF),
        "w_up": w_up.astype(BF), "w_down": w_down.astype(BF),
        "conv_w": conv_w, "conv_b": conv_b.reshape(DEPTH, 1, 2 * D_FF),
    }
    cond = jnp.concatenate([c_ctx[None, :], c, jnp.zeros((8 - 1 - DEC_BATCH, D_MODEL), F32)], axis=0)
    mods = _adaln(cond, w_ada, b_ada)[:, :1 + DEC_BATCH].reshape(DEPTH, 1 + DEC_BATCH, N_MOD, D_MODEL)
    cs, ct, st, ff, gc, gs = _dft_tables()
    rope_tabs = _rope_tables()

    state = {"k": jnp.zeros((BATCH, DEPTH, SEQ, QK_WIDTH), F32),
             "v": jnp.zeros((BATCH, DEPTH, SEQ, V_WIDTH), F32)}

    def in_proj_p(hn, l):
        y, q, state["k"], state["v"] = _in_proj_prompt(hn, w["w_fqkv"], cs, state["k"], state["v"], l)
        return y, q, None

    y_prompt = _trunk(
        x_prompt.reshape(BATCH * SEQ, D_MODEL), mods[:, 0:1], w, SEQ,
        (in_proj_p,
         lambda y: _fourier_prompt(y, ct, st),
         lambda q, kv, l: _attn_prompt(q, state["k"], state["v"], lam_params, subln_g, l)))

    ck = cache_k.reshape(DEC_BATCH, DEPTH, PAST_LEN, QK_WIDTH)
    cv = cache_v.reshape(DEC_BATCH, DEPTH, PAST_LEN, V_WIDTH)

    def in_proj_s(hn, l):
        y, q, k, v = _in_proj_sample(hn, w["w_fqkv"], cs, rope_tabs, l)
        return y, q, (k, v)

    y_sample = _trunk(
        x_sample.reshape(DEC_BATCH * DEC_SEQ, D_MODEL), mods[:, 1:], w, DEC_SEQ,
        (in_proj_s,
         lambda y: _fourier_sample(y, ff, gc, gs),
         lambda q, kv, l: _attn_sample(q, kv[0], kv[1], ck, cv, lam_params, subln_g, l)))

    return (y_prompt.reshape(BATCH, SEQ, D_MODEL),
            y_sample.reshape(DEC_BATCH, DEC_SEQ, D_MODEL),
            state["k"].reshape(BATCH, DEPTH, SEQ, N_HEADS, 2, QK_DIM),
            state["v"].reshape(BATCH, DEPTH, SEQ, N_HEADS, V_DIM))
```

```python
import functools
import math

import jax
import jax.numpy as jnp
from jax import lax
from jax.experimental import pallas as pl
from jax.experimental.pallas import tpu as pltpu

D_MODEL = 1024
BATCH = 32
SEQ = 256
DEPTH = 4
DEC_BATCH = 2
DEC_SEQ = 4096
PAST_LEN = 256
GRID_W = 64
QK_DIM = 64
V_DIM = 2 * QK_DIM
N_HEADS = D_MODEL // (2 * QK_DIM)
FOURIER_GROUPS = 4
FOURIER_WIDTH = D_MODEL // 2
FOURIER_GROUP_DIM = FOURIER_WIDTH // FOURIER_GROUPS
QK_WIDTH = N_HEADS * 2 * QK_DIM
V_WIDTH = N_HEADS * V_DIM
GATE_WIDTH = 2 * D_MODEL
FQKV_WIDTH = FOURIER_WIDTH + 2 * QK_WIDTH + V_WIDTH
D_FF = ((8 * D_MODEL // 3 + 127) // 128) * 128
ROPE_BASE = 10000.0
EPS = 1e-6
N_MOD = 6
SCALE = QK_DIM ** -0.5
LOG2E = math.log2(math.e)

BF = jnp.bfloat16
F32 = jnp.float32

MIB = 1024 * 1024
TM = 512
HALO = 16
FF_CHUNK = 256
ATT_Q = 256
ATT_CK = 256
FP_SEQS = 4
N_SIDE = 64
S1_COLS = 4
S2_ROWS = 8


def _params(n_axes, vmem_mib, flags=None):
    return pltpu.CompilerParams(dimension_semantics=("arbitrary",) * n_axes,
                                vmem_limit_bytes=vmem_mib * MIB, flags=flags)


def _resident(shape):
    zeros = (0,) * len(shape)
    return pl.BlockSpec(shape, lambda *_: zeros, pipeline_mode=pl.Buffered(1))


def _layer_resident(stacked, layer):
    tail = stacked.shape[1:]
    zeros = (0,) * len(tail)
    return pl.BlockSpec((None,) + tail, lambda *_: (layer,) + zeros, pipeline_mode=pl.Buffered(1))


def _dot(a, b):
    return jnp.dot(a, b, preferred_element_type=F32)


def _dot_nt(a, b):
    return lax.dot_general(a, b, (((1,), (1,)), ((), ())), preferred_element_type=F32)


def _norm_mod(x, g, sc, sh):
    y = x * lax.rsqrt(jnp.mean(x * x, axis=-1, keepdims=True) + EPS)
    return (y * g) * (1.0 + sc) + sh


def _adaln_kernel(c_ref, w_ref, b_ref, o_ref):
    c = c_ref[...]
    s = (c * jax.nn.sigmoid(c)).astype(BF)
    o_ref[0] = _dot(s, w_ref[0].astype(BF)) + b_ref[0]


def _adaln(cond, w_ada, b_ada):
    rows = cond.shape[0]
    return pl.pallas_call(
        _adaln_kernel,
        grid=(DEPTH, N_MOD),
        in_specs=[pl.BlockSpec((rows, D_MODEL), lambda l, j: (0, 0)),
                  pl.BlockSpec((1, D_MODEL, D_MODEL), lambda l, j: (l, 0, j)),
                  pl.BlockSpec((1, 1, D_MODEL), lambda l, j: (l, 0, j))],
        out_specs=pl.BlockSpec((1, rows, D_MODEL), lambda l, j: (l, 0, j)),
        out_shape=jax.ShapeDtypeStruct((DEPTH, rows, N_MOD * D_MODEL), F32),
        compiler_params=_params(2, 32),
        name="adaln",
    )(cond, w_ada, b_ada.reshape(DEPTH, 1, N_MOD * D_MODEL))


def _prenorm_kernel(x_ref, g_ref, mods_ref, hn_ref):
    m = mods_ref[0]
    hn_ref[...] = _norm_mod(x_ref[...], g_ref[...], m[1:2], m[0:1]).astype(BF)


def _mods_spec(mods, layer, seq_len):
    _, first, n_rows = mods
    if n_rows == 1:
        index = lambda i: (layer, first, 0, 0)
    else:
        index = lambda i: (layer, first + (i * TM) // seq_len, 0, 0)
    return pl.BlockSpec((None, 1, N_MOD, D_MODEL), index)


def _prenorm(x, g, mods, seq_len):
    m_rows = x.shape[0]
    row = pl.BlockSpec((TM, D_MODEL), lambda i: (i, 0))
    return pl.pallas_call(
        _prenorm_kernel,
        grid=(m_rows // TM,),
        in_specs=[row, _layer_resident(g, 0), _mods_spec(mods, 0, seq_len)],
        out_specs=row,
        out_shape=jax.ShapeDtypeStruct((m_rows, D_MODEL), BF),
        compiler_params=_params(1, 32),
        name="prenorm",
    )(x, g, mods[0])


def _channel_dft(f, cs_ref, y_ref):
    gd = FOURIER_GROUP_DIM
    for g in range(FOURIER_GROUPS):
        yg = _dot(f[:, g * gd:(g + 1) * gd], cs_ref[...])
        y_ref[:, g * gd:(g + 1) * gd] = yg[:, :gd].astype(BF)
        y_ref[:, FOURIER_WIDTH + g * gd:FOURIER_WIDTH + (g + 1) * gd] = yg[:, gd:].astype(BF)


def _rope(x, cos, sin_lo, sin_hi):
    cols = []
    for j in range(x.shape[1] // 128):
        xb = x[:, j * 128:(j + 1) * 128]
        cols.append(xb * cos + pltpu.roll(xb, 16, 1) * sin_hi + pltpu.roll(xb, 112, 1) * sin_lo)
    return jnp.concatenate(cols, axis=1)


def _in_proj_prompt_kernel(hn_ref, w_ref, cs_ref, sk_any, sv_any, y_ref, q_ref, sk_ref, sv_ref):
    del sk_any, sv_any
    hn = hn_ref[...]
    o = FOURIER_WIDTH
    f = _dot(hn, w_ref[:, 0:o]).astype(BF)
    _channel_dft(f, cs_ref, y_ref)
    q_ref[...] = (_dot(hn, w_ref[:, o:o + QK_WIDTH]) * (SCALE * LOG2E)).astype(BF)
    o += QK_WIDTH
    sk_ref[...] = _dot(hn, w_ref[:, o:o + QK_WIDTH]).reshape(sk_ref.shape)
    o += QK_WIDTH
    sv_ref[...] = _dot(hn, w_ref[:, o:o + V_WIDTH]).reshape(sv_ref.shape)


def _in_proj_prompt(hn, w_fqkv, cs, state_k, state_v, layer):
    m_rows = hn.shape[0]
    nb = TM // SEQ
    row = lambda w: pl.BlockSpec((TM, w), lambda i: (i, 0))
    st = pl.BlockSpec((nb, 1, SEQ, D_MODEL), lambda i: (i, layer, 0, 0))
    any_spec = pl.BlockSpec(memory_space=pl.ANY)
    return pl.pallas_call(
        _in_proj_prompt_kernel,
        grid=(m_rows // TM,),
        in_specs=[row(D_MODEL), _layer_resident(w_fqkv, layer), _resident(cs.shape), any_spec, any_spec],
        out_specs=[row(D_MODEL), row(QK_WIDTH), st, st],
        out_shape=[jax.ShapeDtypeStruct((m_rows, D_MODEL), BF),
                   jax.ShapeDtypeStruct((m_rows, QK_WIDTH), BF),
                   jax.ShapeDtypeStruct(state_k.shape, F32),
                   jax.ShapeDtypeStruct(state_v.shape, F32)],
        input_output_aliases={3: 2, 4: 3},
        compiler_params=_params(1, 48),
        name="in_proj_prompt",
    )(hn, w_fqkv, cs, state_k, state_v)


def _in_proj_sample_kernel(hn_ref, w_ref, cs_ref, cos_ref, slo_ref, shi_ref, y_ref, q_ref, k_ref, v_ref):
    hn = hn_ref[...]
    o = FOURIER_WIDTH
    f = _dot(hn, w_ref[:, 0:o]).astype(BF)
    _channel_dft(f, cs_ref, y_ref)
    cos, slo, shi = cos_ref[...], slo_ref[...], shi_ref[...]
    q = _rope(_dot(hn, w_ref[:, o:o + QK_WIDTH]), cos, slo, shi)
    q_ref[...] = (q * (SCALE * LOG2E)).astype(BF)
    o += QK_WIDTH
    k_ref[...] = _rope(_dot(hn, w_ref[:, o:o + QK_WIDTH]), cos, slo, shi).astype(BF)
    o += QK_WIDTH
    v_ref[...] = _dot(hn, w_ref[:, o:o + V_WIDTH]).astype(BF)


def _in_proj_sample(hn, w_fqkv, cs, rope_tabs, layer):
    m_rows = hn.shape[0]
    row = lambda w: pl.BlockSpec((TM, w), lambda i: (i, 0))
    tab = pl.BlockSpec((TM, 128), lambda i: (i % (DEC_SEQ // TM), 0))
    return pl.pallas_call(
        _in_proj_sample_kernel,
        grid=(m_rows // TM,),
        in_specs=[row(D_MODEL), _layer_resident(w_fqkv, layer), _resident(cs.shape), tab, tab, tab],
        out_specs=[row(D_MODEL), row(QK_WIDTH), row(QK_WIDTH), row(V_WIDTH)],
        out_shape=[jax.ShapeDtypeStruct((m_rows, D_MODEL), BF),
                   jax.ShapeDtypeStruct((m_rows, QK_WIDTH), BF),
                   jax.ShapeDtypeStruct((m_rows, QK_WIDTH), BF),
                   jax.ShapeDtypeStruct((m_rows, V_WIDTH), BF)],
        compiler_params=_params(1, 48),
        name="in_proj_sample",
    )(hn, w_fqkv, cs, *rope_tabs)


def _fourier_prompt_kernel(y_ref, ct_ref, st_ref, fr_ref):
    for b in range(FP_SEQS):
        rows = slice(b * SEQ, (b + 1) * SEQ)
        fr = _dot(ct_ref[...], y_ref[rows, :FOURIER_WIDTH]) + _dot(st_ref[...], y_ref[rows, FOURIER_WIDTH:])
        fr_ref[rows, :] = fr.astype(BF)


def _fourier_prompt(y, ct, st):
    m_rows = y.shape[0]
    return pl.pallas_call(
        _fourier_prompt_kernel,
        grid=(m_rows // (FP_SEQS * SEQ),),
        in_specs=[pl.BlockSpec((FP_SEQS * SEQ, D_MODEL), lambda i: (i, 0)), _resident(ct.shape),
                  _resident(st.shape)],
        out_specs=pl.BlockSpec((FP_SEQS * SEQ, FOURIER_WIDTH), lambda i: (i, 0)),
        out_shape=jax.ShapeDtypeStruct((m_rows, FOURIER_WIDTH), BF),
        compiler_params=_params(1, 32),
        name="fourier_prompt",
    )(y, ct, st)


def _fourier_stage1_kernel(y_ref, ff_ref, a_ref):
    pq = _dot(ff_ref[...], y_ref[0])
    p, q = pq[:N_SIDE], pq[N_SIDE:]
    w = FOURIER_WIDTH
    for s in range(S1_COLS):
        c = s * 2 * w
        a_ref[0, :, c:c + w] = (p[:, c:c + w] + q[:, c + w:c + 2 * w]).astype(BF)
        a_ref[0, :, c + w:c + 2 * w] = (p[:, c + w:c + 2 * w] - q[:, c:c + w]).astype(BF)


def _fourier_stage2_kernel(a_ref, gc_ref, gs_ref, fr_ref):
    w = FOURIER_WIDTH
    for kk in range(S2_ROWS):
        a = a_ref[0, kk * N_SIDE:(kk + 1) * N_SIDE, :]
        xr = _dot(gc_ref[kk], a[:, :w]) + _dot(gs_ref[kk], a[:, w:])
        fr_ref[0, :, kk * w:(kk + 1) * w] = xr.astype(BF)


def _fourier_sample(y, ff, gc, gs):
    nb = y.shape[0] // DEC_SEQ
    row_len = N_SIDE * D_MODEL
    y3 = y.reshape(nb, N_SIDE, row_len)
    cols = S1_COLS * D_MODEL
    a = pl.pallas_call(
        _fourier_stage1_kernel,
        grid=(nb, row_len // cols),
        in_specs=[pl.BlockSpec((1, N_SIDE, cols), lambda b, j: (b, 0, j)),
                  pl.BlockSpec(ff.shape, lambda b, j: (0, 0))],
        out_specs=pl.BlockSpec((1, N_SIDE, cols), lambda b, j: (b, 0, j)),
        out_shape=jax.ShapeDtypeStruct((nb, N_SIDE, row_len), BF),
        compiler_params=_params(2, 32),
        name="fourier_stage1",
    )(y3, ff)
    a = a.reshape(nb, N_SIDE * N_SIDE, D_MODEL)
    fr = pl.pallas_call(
        _fourier_stage2_kernel,
        grid=(nb, N_SIDE // S2_ROWS),
        in_specs=[pl.BlockSpec((1, S2_ROWS * N_SIDE, D_MODEL), lambda b, j: (b, j, 0)),
                  pl.BlockSpec((S2_ROWS, N_SIDE, N_SIDE), lambda b, j: (j, 0, 0)),
                  pl.BlockSpec((S2_ROWS, N_SIDE, N_SIDE), lambda b, j: (j, 0, 0))],
        out_specs=pl.BlockSpec((1, N_SIDE, S2_ROWS * FOURIER_WIDTH), lambda b, j: (b, 0, j)),
        out_shape=jax.ShapeDtypeStruct((nb, N_SIDE, N_SIDE * FOURIER_WIDTH), BF),
        compiler_params=_params(2, 32),
        name="fourier_stage2",
    )(a, gc, gs)
    return fr.reshape(nb * DEC_SEQ, FOURIER_WIDTH)


def _lam(lam_ref, lam_init):
    lp = lam_ref[0]
    a = jnp.sum(lp[0:1] * lp[1:2], axis=(0, 1), keepdims=True)
    b = jnp.sum(lp[2:3] * lp[3:4], axis=(0, 1), keepdims=True)
    return jnp.exp(a) - jnp.exp(b) + lam_init


def _split_maps(q):
    lane = lax.broadcasted_iota(jnp.int32, q.shape, 1)
    zero = jnp.zeros_like(q)
    return jnp.where(lane < QK_DIM, q, zero), jnp.where(lane >= QK_DIM, q, zero)


def _attn_prompt_kernel(q_ref, k_ref, v_ref, lam_ref, g_ref, o_ref, s_scr0, s_scr1, *, lam_init):
    lam = _lam(lam_ref, lam_init)
    s_scr = (s_scr0, s_scr1)

    def scores(h):
        sl = slice(h * V_DIM, (h + 1) * V_DIM)
        k = k_ref[0, 0, :, sl].astype(BF)
        maxes = []
        for mp, qz in enumerate(_split_maps(q_ref[:, sl])):
            s = _dot_nt(k, qz)
            s_scr[h % 2][mp] = s
            run = s[0:8, :]
            for r in range(1, SEQ // 8):
                run = jnp.maximum(run, s[r * 8:(r + 1) * 8, :])
            maxes.append(jnp.broadcast_to(jnp.max(run, axis=0, keepdims=True), (8, SEQ)))
        return maxes

    def values(h, maxes):
        sl = slice(h * V_DIM, (h + 1) * V_DIM)
        vt = v_ref[0, 0, :, sl].T.astype(BF)
        acc, l = [], []
        for mp in range(2):
            run = jnp.zeros((8, SEQ), F32)
            strips = []
            for r in range(SEQ // 16):
                p_lo = jnp.exp2(s_scr[h % 2][mp, r * 16:r * 16 + 8, :] - maxes[mp])
                p_hi = jnp.exp2(s_scr[h % 2][mp, r * 16 + 8:r * 16 + 16, :] - maxes[mp])
                run = run + p_lo + p_hi
                strips.append(jnp.concatenate([p_lo, p_hi], axis=0).astype(BF))
            l.append(jnp.sum(run, axis=0, keepdims=True))
            acc.append(_dot(vt, jnp.concatenate(strips, axis=0)))
        ot = acc[0] - (lam * l[0] / l[1]) * acc[1]
        ot = ot * lax.rsqrt(jnp.mean(ot * ot, axis=0, keepdims=True) + EPS * l[0] * l[0])
        o_ref[:, sl] = (ot.T * g_ref[0] * (1.0 - lam_init)).astype(BF)

    maxes = scores(0)
    for h in range(N_HEADS):
        following = scores(h + 1) if h + 1 < N_HEADS else None
        values(h, maxes)
        maxes = following


def _attn_prompt(q, state_k, state_v, lam_params, subln_g, layer):
    m_rows = q.shape[0]
    lam_init = 0.8 - 0.6 * math.exp(-0.3 * layer)
    st = pl.BlockSpec((1, 1, SEQ, D_MODEL), lambda b: (b, layer, 0, 0))
    return pl.pallas_call(
        functools.partial(_attn_prompt_kernel, lam_init=lam_init),
        grid=(m_rows // SEQ,),
        in_specs=[pl.BlockSpec((SEQ, QK_WIDTH), lambda b: (b, 0)), st, st,
                  pl.BlockSpec((1, 4, QK_DIM), lambda b: (layer, 0, 0)),
                  pl.BlockSpec((1, 1, V_DIM), lambda b: (layer, 0, 0))],
        out_specs=pl.BlockSpec((SEQ, V_WIDTH), lambda b: (b, 0)),
        out_shape=jax.ShapeDtypeStruct((m_rows, V_WIDTH), BF),
        scratch_shapes=[pltpu.VMEM((2, SEQ, SEQ), F32), pltpu.VMEM((2, SEQ, SEQ), F32)],
        compiler_params=_params(1, 32),
        name="attn_prompt",
    )(q, state_k, state_v, lam_params, subln_g.reshape(DEPTH, 1, V_DIM))


def _attn_sample_kernel(q_ref, kn_ref, kc_ref, vn_ref, vc_ref, lam_ref, g_ref, o_ref,
                        qz_scr, vt_new, vt_old, s_new0, s_new1, s_old0, s_old1, m_scr0, m_scr1,
                        acc_scr0, acc_scr1, l_scr0, l_scr1, *, lam_init, steps_per_head):
    s_new, s_old, m_scr = (s_new0, s_new1), (s_old0, s_old1), (m_scr0, m_scr1)
    acc_scr, l_scr = (acc_scr0, acc_scr1), (l_scr0, l_scr1)
    g = pl.program_id(0)

    @pl.when(g == 0)
    def _():
        for ref in (s_new1, s_old1, m_scr1, acc_scr1):
            ref[...] = jnp.zeros_like(ref)
        l_scr1[...] = jnp.ones_like(l_scr1)

    @pl.when(jnp.maximum(g - 1, 0) % steps_per_head == 0)
    def _():
        for c in range(DEC_SEQ // ATT_CK):
            vt_new[c] = vn_ref[c * ATT_CK:(c + 1) * ATT_CK, :].astype(F32).T.astype(BF)
        vt_old[...] = vc_ref[0, 0].T.astype(BF)

    def step(cur):
        prv = 1 - cur
        l1 = jnp.sum(l_scr[prv][0], axis=0, keepdims=True)
        l2 = jnp.sum(l_scr[prv][1], axis=0, keepdims=True)
        ot = acc_scr[prv][0] - (_lam(lam_ref, lam_init) * l1 / l2) * acc_scr[prv][1]
        ot = ot * lax.rsqrt(jnp.mean(ot * ot, axis=0, keepdims=True) + EPS * l1 * l1)
        o_ref[...] = (ot.T * g_ref[0] * (1.0 - lam_init)).astype(BF)

        q1, q2 = _split_maps(q_ref[...])
        qz_scr[0] = q1
        qz_scr[1] = q2

        def stages(k_c, vt_c, s_dst, s_src, stats):
            mx, l, acc = stats
            new_mx, new_l, new_acc = [], [], []
            n_keys = k_c.shape[0]
            for mp in range(2):
                s = _dot_nt(k_c, qz_scr[mp])
                s_dst[mp] = s
                run = mx[mp]
                for r in range(n_keys // 8):
                    run = jnp.maximum(run, s[r * 8:(r + 1) * 8, :])
                new_mx.append(run)
            for mp in range(2):
                m8 = m_scr[prv][mp]
                run = l[mp]
                strips = []
                for r in range(n_keys // 16):
                    p_lo = jnp.exp2(s_src[mp, r * 16:r * 16 + 8, :] - m8)
                    p_hi = jnp.exp2(s_src[mp, r * 16 + 8:r * 16 + 16, :] - m8)
                    run = run + p_lo + p_hi
                    strips.append(jnp.concatenate([p_lo, p_hi], axis=0).astype(BF))
                new_l.append(run)
                new_acc.append(acc[mp] + _dot(vt_c, jnp.concatenate(strips, axis=0)))
            return tuple(new_mx), tuple(new_l), tuple(new_acc)

        neg = jnp.full((8, ATT_Q), -jnp.inf, F32)
        zero = jnp.zeros((8, ATT_Q), F32)
        zero_acc = jnp.zeros((V_DIM, ATT_Q), F32)
        stats = stages(kc_ref[0, 0].astype(BF), vt_old[...], s_old[cur], s_old[prv],
                       ((neg, neg), (zero, zero), (zero_acc, zero_acc)))
        for c in range(DEC_SEQ // ATT_CK):
            stats = stages(kn_ref[c * ATT_CK:(c + 1) * ATT_CK, :], vt_new[c], s_new[cur].at[c],
                           s_new[prv].at[c], stats)
        mx, l, acc = stats

        for mp in range(2):
            m_scr[cur][mp] = jnp.broadcast_to(jnp.max(mx[mp], axis=0, keepdims=True), (8, ATT_Q))
            acc_scr[cur][mp] = acc[mp]
            l_scr[cur][mp] = l[mp]

    pl.when(g % 2 == 0)(lambda: step(0))
    pl.when(g % 2 == 1)(lambda: step(1))


def _attn_sample(q, k, v, cache_k, cache_v, lam_params, subln_g, layer):
    m_rows = q.shape[0]
    nq = DEC_SEQ // ATT_Q
    n_blocks = (m_rows // DEC_SEQ) * N_HEADS * nq
    nch = DEC_SEQ // ATT_CK
    lam_init = 0.8 - 0.6 * math.exp(-0.3 * layer)

    def decode(g, lag):
        n = jnp.clip(g - lag, 0, n_blocks - 1)
        return n // (N_HEADS * nq), (n // nq) % N_HEADS, n % nq

    def blk(lag):
        def index(g):
            b, h, i = decode(g, lag)
            return b * nq + i, h
        return pl.BlockSpec((ATT_Q, V_DIM), index)

    def new(lag):
        return pl.BlockSpec((DEC_SEQ, V_DIM), lambda g: decode(g, lag)[:2])

    def old(lag):
        def index(g):
            b, h, _ = decode(g, lag)
            return b, layer, 0, h
        return pl.BlockSpec((1, 1, PAST_LEN, V_DIM), index)

    per_slot = [pltpu.VMEM((nch, 2, ATT_CK, ATT_Q), F32), pltpu.VMEM((2, PAST_LEN, ATT_Q), F32),
                pltpu.VMEM((2, 8, ATT_Q), F32), pltpu.VMEM((2, V_DIM, ATT_Q), F32),
                pltpu.VMEM((2, 8, ATT_Q), F32)]
    return pl.pallas_call(
        functools.partial(_attn_sample_kernel, lam_init=lam_init, steps_per_head=nq),
        grid=(n_blocks + 2,),
        in_specs=[blk(0), new(0), old(0), new(1), old(1),
                  pl.BlockSpec((1, 4, QK_DIM), lambda g: (layer, 0, 0)),
                  pl.BlockSpec((1, 1, V_DIM), lambda g: (layer, 0, 0))],
        out_specs=blk(2),
        out_shape=jax.ShapeDtypeStruct((m_rows, V_WIDTH), BF),
        scratch_shapes=[pltpu.VMEM((2, ATT_Q, V_DIM), BF), pltpu.VMEM((nch, V_DIM, ATT_CK), BF),
                        pltpu.VMEM((V_DIM, PAST_LEN), BF)]
                       + [shape for shape in per_slot for _ in range(2)],
        compiler_params=_params(1, 56),
        name="attn_sample",
    )(q, k, cache_k, v, cache_v, lam_params, subln_g.reshape(DEPTH, 1, V_DIM))


def _mix_out_kernel(hn_ref, fr_ref, o_ref, x_ref, wg_ref, wf_ref, wa_ref, wo_ref, mods_ref, g2_ref,
                    xo_ref, hno_ref):
    hn = hn_ref[...]
    a_four = _dot(fr_ref[...], wf_ref[...])
    mixed = jax.nn.sigmoid(_dot(hn, wg_ref[:, :D_MODEL])) * a_four
    a_attn = _dot(o_ref[...], wa_ref[...])
    mixed += jax.nn.sigmoid(_dot(hn, wg_ref[:, D_MODEL:])) * a_attn
    mix = _dot(mixed.astype(BF), wo_ref[...])
    m = mods_ref[0]
    x = x_ref[...] + m[2:3] * mix
    xo_ref[...] = x
    hno_ref[...] = _norm_mod(x, g2_ref[...], m[4:5], m[3:4]).astype(BF)


def _mix_out(hn, fr, o, x, w_gate, w_four, w_attn, w_o, mods, norm2_g, seq_len, layer):
    m_rows = hn.shape[0]
    row = lambda w: pl.BlockSpec((TM, w), lambda i: (i, 0))
    return pl.pallas_call(
        _mix_out_kernel,
        grid=(m_rows // TM,),
        in_specs=[row(D_MODEL), row(FOURIER_WIDTH), row(V_WIDTH), row(D_MODEL),
                  _layer_resident(w_gate, layer), _layer_resident(w_four, layer),
                  _layer_resident(w_attn, layer), _layer_resident(w_o, layer),
                  _mods_spec(mods, layer, seq_len), _layer_resident(norm2_g, layer)],
        out_specs=[row(D_MODEL), row(D_MODEL)],
        out_shape=[jax.ShapeDtypeStruct((m_rows, D_MODEL), F32),
                   jax.ShapeDtypeStruct((m_rows, D_MODEL), BF)],
        compiler_params=_params(1, 56),
        name="mix_out",
    )(hn, fr, o, x, w_gate, w_four, w_attn, w_o, mods[0], norm2_g)


def _ffn_kernel(hp_ref, hn_ref, hx_ref, x_ref, wup_ref, cw_ref, cb_ref, wdn_ref, mods_ref, gn_ref,
                modsn_ref, *rest, seq_len, last):
    if last:
        y_ref, lhs_scr, u_scr, h_scr = rest
    else:
        xo_ref, hno_ref, lhs_scr, u_scr, h_scr = rest
    i = pl.program_id(0)
    if seq_len % TM == 0:
        tiles = seq_len // TM
        blank = jnp.zeros(hp_ref.shape, hp_ref.dtype)
        hp = jnp.where(i % tiles == 0, blank, hp_ref[...])
        hx = jnp.where(i % tiles == tiles - 1, blank, hx_ref[...])
        edge = lambda rows, shift: rows
    else:
        hp, hx = hp_ref[...], hx_ref[...]
        pos = (i * TM + lax.broadcasted_iota(jnp.int32, (TM, 1), 0)) % seq_len
        keep = {-1: pos != 0, 1: pos != seq_len - 1}
        edge = lambda rows, shift: jnp.where(keep[shift], rows, 0.0)
    lhs_scr[0:HALO] = hp
    lhs_scr[HALO:HALO + TM] = hn_ref[...]
    lhs_scr[HALO + TM:] = hx
    lhs = lhs_scr[...]

    def conv(col, slot):
        u_scr[slot] = _dot(lhs, wup_ref[:, col:col + FF_CHUNK])
        prev = edge(u_scr[slot, pl.ds(HALO - 1, TM), :], -1)
        cur = u_scr[slot, pl.ds(HALO, TM), :]
        nxt = edge(u_scr[slot, pl.ds(HALO + 1, TM), :], 1)
        w = cw_ref[:, col:col + FF_CHUNK]
        return prev * w[0:1] + cur * w[1:2] + nxt * w[2:3] + cb_ref[:, col:col + FF_CHUNK]

    for c in range(D_FF // FF_CHUNK):
        val = conv(c * FF_CHUNK, 0)
        gate = conv(D_FF + c * FF_CHUNK, 1)
        h_scr[:, c * FF_CHUNK:(c + 1) * FF_CHUNK] = (gate * jax.nn.sigmoid(gate) * val).astype(BF)

    m = mods_ref[0]
    x = x_ref[...] + m[5:6] * _dot(h_scr[...], wdn_ref[...])
    if last:
        y_ref[...] = x * lax.rsqrt(jnp.mean(x * x, axis=-1, keepdims=True) + EPS) * gn_ref[...]
    else:
        xo_ref[...] = x
        mn = modsn_ref[0]
        hno_ref[...] = _norm_mod(x, gn_ref[...], mn[1:2], mn[0:1]).astype(BF)


def _ffn(hn, x, w_up, conv_w, conv_b, w_down, mods, norm1_g, final_g, seq_len, layer):
    m_rows = hn.shape[0]
    last = layer == DEPTH - 1
    g_next, g_spec = (final_g, _resident(final_g.shape)) if last else (norm1_g, _layer_resident(norm1_g, layer + 1))
    per = TM // HALO
    row = lambda w: pl.BlockSpec((TM, w), lambda i: (i, 0))
    prev = pl.BlockSpec((HALO, D_MODEL), lambda i: (jnp.maximum(i * per - 1, 0), 0))
    nxt = pl.BlockSpec((HALO, D_MODEL), lambda i: (jnp.minimum((i + 1) * per, m_rows // HALO - 1), 0))
    if last:
        out_specs = row(D_MODEL)
        out_shape = jax.ShapeDtypeStruct((m_rows, D_MODEL), F32)
    else:
        out_specs = [row(D_MODEL), row(D_MODEL)]
        out_shape = [jax.ShapeDtypeStruct((m_rows, D_MODEL), F32),
                     jax.ShapeDtypeStruct((m_rows, D_MODEL), BF)]
    return pl.pallas_call(
        functools.partial(_ffn_kernel, seq_len=seq_len, last=last),
        grid=(m_rows // TM,),
        in_specs=[prev, row(D_MODEL), nxt, row(D_MODEL),
                  _layer_resident(w_up, layer), _layer_resident(conv_w, layer),
                  _layer_resident(conv_b, layer), _layer_resident(w_down, layer),
                  _mods_spec(mods, layer, seq_len), g_spec,
                  _mods_spec(mods, min(layer + 1, DEPTH - 1), seq_len)],
        out_specs=out_specs,
        out_shape=out_shape,
        scratch_shapes=[pltpu.VMEM((TM + 2 * HALO, D_MODEL), BF),
                        pltpu.VMEM((2, TM + 2 * HALO, FF_CHUNK), F32),
                        pltpu.VMEM((TM, D_FF), BF)],
        compiler_params=_params(1, 56),
        name="ffn",
    )(hn, hn, hn, x, w_up, conv_w, conv_b, w_down, mods[0], g_next, mods[0])


def _cos_sin(num, den):
    ang = (2.0 * math.pi / den) * (num % den).astype(F32)
    return jnp.cos(ang), jnp.sin(ang)


def _dft_tables():
    n = jnp.arange(FOURIER_GROUP_DIM)
    c, s = _cos_sin(n[:, None] * n[None, :], FOURIER_GROUP_DIM)
    cs = (jnp.concatenate([c, -s], axis=1) * FOURIER_GROUP_DIM ** -0.5).astype(BF)
    t = jnp.arange(SEQ)
    c, s = _cos_sin(t[:, None] * t[None, :], SEQ)
    ct, st = (c * SEQ ** -0.5).astype(BF), (s * SEQ ** -0.5).astype(BF)
    r = jnp.arange(N_SIDE)
    c, s = _cos_sin(r[:, None] * r[None, :], N_SIDE)
    ff = (jnp.concatenate([c, s], axis=0) * N_SIDE ** -0.5).astype(BF)
    k = r[:, None, None] + N_SIDE * r[None, :, None]
    c, s = _cos_sin(k * r[None, None, :], N_SIDE * N_SIDE)
    gc, gs = (c * N_SIDE ** -0.5).astype(BF), (s * N_SIDE ** -0.5).astype(BF)
    return cs, ct, st, ff, gc, gs


def _rope_tables():
    half = QK_DIM // 2
    inv_freq = 1.0 / (ROPE_BASE ** (jnp.arange(0, half, 2, dtype=F32) / half))
    t = jnp.arange(DEC_SEQ)
    row = (t // GRID_W).astype(F32)
    col = (t % GRID_W).astype(F32)
    lane = jnp.arange(128)
    d = lane % QK_DIM
    freq = inv_freq[d % (half // 2)]
    ang = jnp.where((d < half)[None, :], row[:, None] * freq[None, :], col[:, None] * freq[None, :])
    cos, sin = jnp.cos(ang), jnp.sin(ang)
    upper = ((d % half) >= half // 2)[None, :]
    sin_hi = jnp.where(upper, sin, 0.0)
    sin_lo = jnp.where(upper, 0.0, -sin)
    return cos, sin_lo, sin_hi


def _trunk(x, mods, w, seq_len, layer_fns):
    in_proj, fourier, attention = layer_fns
    hn = _prenorm(x, w["norm1_g"], mods, seq_len)
    for l in range(DEPTH):
        y, q, kv = in_proj(hn, l)
        fr = fourier(y)
        o = attention(q, kv, l)
        x, hn2 = _mix_out(hn, fr, o, x, w["w_gate"], w["w_fourier"], w["w_attn"], w["w_o"],
                          mods, w["norm2_g"], seq_len, l)
        res = _ffn(hn2, x, w["w_up"], w["conv_w"], w["conv_b"], w["w_down"],
                   mods, w["norm1_g"], w["final_g"], seq_len, l)
        if l == DEPTH - 1:
            return res
        x, hn = res


def kernel(x_prompt, x_sample, c, cache_k, cache_v, c_ctx, norm1_g, norm2_g, final_g, w_ada, b_ada, w_in,
           w_fourier, lam_params, subln_g, w_attn, w_o, w_up, conv_w, conv_b, w_down):
    w = {
        "norm1_g": norm1_g.reshape(DEPTH, 1, D_MODEL), "norm2_g": norm2_g.reshape(DEPTH, 1, D_MODEL),
        "final_g": final_g.reshape(1, D_MODEL),
        "w_fqkv": w_in[:, :, :FQKV_WIDTH].astype(BF), "w_gate": w_in[:, :, FQKV_WIDTH:].astype(BF),
        "w_fourier": w_fourier.astype(BF), "w_attn": w_attn.astype(BF), "w_o": w_o.astype(BF),
        "w_up": w_up.astype(BF), "w_down": w_down.astype(BF),
        "conv_w": conv_w, "conv_b": conv_b.reshape(DEPTH, 1, 2 * D_FF),
    }
    cond = jnp.concatenate([c_ctx[None, :], c, jnp.zeros((8 - 1 - DEC_BATCH, D_MODEL), F32)], axis=0)
    mods = _adaln(cond, w_ada, b_ada).reshape(DEPTH, cond.shape[0], N_MOD, D_MODEL)
    cs, ct, st, ff, gc, gs = _dft_tables()
    rope_tabs = _rope_tables()

    state = {"k": jnp.zeros((BATCH, DEPTH, SEQ, QK_WIDTH), F32),
             "v": jnp.zeros((BATCH, DEPTH, SEQ, V_WIDTH), F32)}

    def in_proj_p(hn, l):
        y, q, state["k"], state["v"] = _in_proj_prompt(hn, w["w_fqkv"], cs, state["k"], state["v"], l)
        return y, q, None

    y_prompt = _trunk(
        x_prompt.reshape(BATCH * SEQ, D_MODEL), (mods, 0, 1), w, SEQ,
        (in_proj_p,
         lambda y: _fourier_prompt(y, ct, st),
         lambda q, kv, l: _attn_prompt(q, state["k"], state["v"], lam_params, subln_g, l)))

    ck = cache_k.reshape(DEC_BATCH, DEPTH, PAST_LEN, QK_WIDTH)
    cv = cache_v.reshape(DEC_BATCH, DEPTH, PAST_LEN, V_WIDTH)

    def in_proj_s(hn, l):
        y, q, k, v = _in_proj_sample(hn, w["w_fqkv"], cs, rope_tabs, l)
        return y, q, (k, v)

    y_sample = _trunk(
        x_sample.reshape(DEC_BATCH * DEC_SEQ, D_MODEL), (mods, 1, DEC_BATCH), w, DEC_SEQ,
        (in_proj_s,
         lambda y: _fourier_sample(y, ff, gc, gs),
         lambda q, kv, l: _attn_sample(q, kv[0], kv[1], ck, cv, lam_params, subln_g, l)))

    return (y_prompt.reshape(BATCH, SEQ, D_MODEL),
            y_sample.reshape(DEC_BATCH, DEC_SEQ, D_MODEL),
            state["k"].reshape(BATCH, DEPTH, SEQ, N_HEADS, 2, QK_DIM),
            state["v"].reshape(BATCH, DEPTH, SEQ, N_HEADS, V_DIM))
```

```python
import functools
import math

import jax
import jax.numpy as jnp
from jax import lax
from jax.experimental import pallas as pl
from jax.experimental.pallas import tpu as pltpu

D_MODEL = 1024
BATCH = 32
SEQ = 256
DEPTH = 4
DEC_BATCH = 2
DEC_SEQ = 4096
PAST_LEN = 256
GRID_W = 64
QK_DIM = 64
V_DIM = 2 * QK_DIM
N_HEADS = D_MODEL // (2 * QK_DIM)
FOURIER_GROUPS = 4
FOURIER_WIDTH = D_MODEL // 2
FOURIER_GROUP_DIM = FOURIER_WIDTH // FOURIER_GROUPS
QK_WIDTH = N_HEADS * 2 * QK_DIM
V_WIDTH = N_HEADS * V_DIM
GATE_WIDTH = 2 * D_MODEL
FQKV_WIDTH = FOURIER_WIDTH + 2 * QK_WIDTH + V_WIDTH
D_FF = ((8 * D_MODEL // 3 + 127) // 128) * 128
ROPE_BASE = 10000.0
EPS = 1e-6
N_MOD = 6
SCALE = QK_DIM ** -0.5
LOG2E = math.log2(math.e)

BF = jnp.bfloat16
F32 = jnp.float32

MIB = 1024 * 1024
TM = 512
TAIL_PARTS = 2
HALO = 16
FF_CHUNK = 256
ATT_Q = 256
ATT_CK = 256
N_SIDE = 64
S1_COLS = 8
S2_ROWS = 16


def _params(n_axes, vmem_mib, flags=None):
    return pltpu.CompilerParams(dimension_semantics=("arbitrary",) * n_axes,
                                vmem_limit_bytes=vmem_mib * MIB, flags=flags)


def _resident(shape):
    zeros = (0,) * len(shape)
    return pl.BlockSpec(shape, lambda *_: zeros, pipeline_mode=pl.Buffered(1))


def _layer_resident(stacked, layer):
    tail = stacked.shape[1:]
    zeros = (0,) * len(tail)
    return pl.BlockSpec((None,) + tail, lambda *_: (layer,) + zeros, pipeline_mode=pl.Buffered(1))


def _dot(a, b):
    return jnp.dot(a, b, preferred_element_type=F32)


def _dot_nt(a, b):
    return lax.dot_general(a, b, (((1,), (1,)), ((), ())), preferred_element_type=F32)


def _row_parts():
    return [slice(p * TM // TAIL_PARTS, (p + 1) * TM // TAIL_PARTS) for p in range(TAIL_PARTS)]


def _norm_mod(x, g, sc, sh):
    y = x * lax.rsqrt(jnp.mean(x * x, axis=-1, keepdims=True) + EPS)
    return (y * g) * (1.0 + sc) + sh


def _adaln_kernel(c_ref, w_ref, b_ref, o_ref):
    c = c_ref[...]
    s = (c * jax.nn.sigmoid(c)).astype(BF)
    o_ref[0] = _dot(s, w_ref[0].astype(BF)) + b_ref[0]


def _adaln(cond, w_ada, b_ada):
    rows = cond.shape[0]
    return pl.pallas_call(
        _adaln_kernel,
        grid=(DEPTH, N_MOD),
        in_specs=[pl.BlockSpec((rows, D_MODEL), lambda l, j: (0, 0)),
                  pl.BlockSpec((1, D_MODEL, D_MODEL), lambda l, j: (l, 0, j)),
                  pl.BlockSpec((1, 1, D_MODEL), lambda l, j: (l, 0, j))],
        out_specs=pl.BlockSpec((1, rows, D_MODEL), lambda l, j: (l, 0, j)),
        out_shape=jax.ShapeDtypeStruct((DEPTH, rows, N_MOD * D_MODEL), F32),
        compiler_params=_params(2, 32),
        name="adaln",
    )(cond, w_ada, b_ada.reshape(DEPTH, 1, N_MOD * D_MODEL))


def _prenorm_kernel(x_ref, g_ref, mods_ref, hn_ref):
    m = mods_ref[0]
    hn_ref[...] = _norm_mod(x_ref[...], g_ref[...], m[1:2], m[0:1]).astype(BF)


def _mods_spec(mods, layer, seq_len):
    _, first, n_rows = mods
    if n_rows == 1:
        index = lambda i: (layer, first, 0, 0)
    else:
        index = lambda i: (layer, first + (i * TM) // seq_len, 0, 0)
    return pl.BlockSpec((None, 1, N_MOD, D_MODEL), index)


def _prenorm(x, g, mods, seq_len):
    m_rows = x.shape[0]
    row = pl.BlockSpec((TM, D_MODEL), lambda i: (i, 0))
    return pl.pallas_call(
        _prenorm_kernel,
        grid=(m_rows // TM,),
        in_specs=[row, _layer_resident(g, 0), _mods_spec(mods, 0, seq_len)],
        out_specs=row,
        out_shape=jax.ShapeDtypeStruct((m_rows, D_MODEL), BF),
        compiler_params=_params(1, 32),
        name="prenorm",
    )(x, g, mods[0])


def _channel_dft(f, cs_ref, y_ref):
    gd = FOURIER_GROUP_DIM
    for g in range(FOURIER_GROUPS):
        yg = _dot(f[:, g * gd:(g + 1) * gd], cs_ref[...])
        y_ref[:, g * gd:(g + 1) * gd] = yg[:, :gd].astype(BF)
        y_ref[:, FOURIER_WIDTH + g * gd:FOURIER_WIDTH + (g + 1) * gd] = yg[:, gd:].astype(BF)


def _rope(x, cos, sin_lo, sin_hi):
    cols = []
    for j in range(x.shape[1] // 128):
        xb = x[:, j * 128:(j + 1) * 128]
        cols.append(xb * cos + pltpu.roll(xb, 16, 1) * sin_hi + pltpu.roll(xb, 112, 1) * sin_lo)
    return jnp.concatenate(cols, axis=1)


def _in_proj_prompt_kernel(hn_ref, w_ref, cs_ref, ct_ref, st_ref, sk_any, sv_any, fr_ref, q_ref, sk_ref, sv_ref,
                           y_scr):
    del sk_any, sv_any
    hn = hn_ref[...]
    o = FOURIER_WIDTH
    f = _dot(hn, w_ref[:, 0:o]).astype(BF)
    _channel_dft(f, cs_ref, y_scr)
    for b in range(TM // SEQ):
        rows = slice(b * SEQ, (b + 1) * SEQ)
        fr = _dot(ct_ref[...], y_scr[rows, :o]) + _dot(st_ref[...], y_scr[rows, o:])
        fr_ref[rows, :] = fr.astype(BF)
    q_ref[...] = (_dot(hn, w_ref[:, o:o + QK_WIDTH]) * (SCALE * LOG2E)).astype(BF)
    o += QK_WIDTH
    sk_ref[...] = _dot(hn, w_ref[:, o:o + QK_WIDTH]).reshape(sk_ref.shape)
    o += QK_WIDTH
    sv_ref[...] = _dot(hn, w_ref[:, o:o + V_WIDTH]).reshape(sv_ref.shape)


def _in_proj_prompt(hn, w_fqkv, cs, ct, st_tab, state_k, state_v, layer):
    m_rows = hn.shape[0]
    nb = TM // SEQ
    row = lambda w: pl.BlockSpec((TM, w), lambda i: (i, 0))
    st = pl.BlockSpec((nb, 1, SEQ, D_MODEL), lambda i: (i, layer, 0, 0))
    any_spec = pl.BlockSpec(memory_space=pl.ANY)
    return pl.pallas_call(
        _in_proj_prompt_kernel,
        grid=(m_rows // TM,),
        in_specs=[row(D_MODEL), _layer_resident(w_fqkv, layer), _resident(cs.shape), _resident(ct.shape),
                  _resident(st_tab.shape), any_spec, any_spec],
        out_specs=[row(FOURIER_WIDTH), row(QK_WIDTH), st, st],
        out_shape=[jax.ShapeDtypeStruct((m_rows, FOURIER_WIDTH), BF),
                   jax.ShapeDtypeStruct((m_rows, QK_WIDTH), BF),
                   jax.ShapeDtypeStruct(state_k.shape, F32),
                   jax.ShapeDtypeStruct(state_v.shape, F32)],
        input_output_aliases={5: 2, 6: 3},
        scratch_shapes=[pltpu.VMEM((TM, D_MODEL), BF)],
        compiler_params=_params(1, 48),
        name="in_proj_prompt",
    )(hn, w_fqkv, cs, ct, st_tab, state_k, state_v)


def _in_proj_sample_kernel(hn_ref, w_ref, cs_ref, cos_ref, slo_ref, shi_ref, y_ref, q_ref, k_ref, v_ref):
    hn = hn_ref[...]
    o = FOURIER_WIDTH
    f = _dot(hn, w_ref[:, 0:o]).astype(BF)
    _channel_dft(f, cs_ref, y_ref)
    cos, slo, shi = cos_ref[...], slo_ref[...], shi_ref[...]
    q = _rope(_dot(hn, w_ref[:, o:o + QK_WIDTH]), cos, slo, shi)
    q_ref[...] = (q * (SCALE * LOG2E)).astype(BF)
    o += QK_WIDTH
    k_ref[...] = _rope(_dot(hn, w_ref[:, o:o + QK_WIDTH]), cos, slo, shi).astype(BF)
    o += QK_WIDTH
    v_ref[...] = _dot(hn, w_ref[:, o:o + V_WIDTH]).astype(BF)


def _in_proj_sample(hn, w_fqkv, cs, rope_tabs, layer):
    m_rows = hn.shape[0]
    row = lambda w: pl.BlockSpec((TM, w), lambda i: (i, 0))
    tab = pl.BlockSpec((TM, 128), lambda i: (i % (DEC_SEQ // TM), 0))
    return pl.pallas_call(
        _in_proj_sample_kernel,
        grid=(m_rows // TM,),
        in_specs=[row(D_MODEL), _layer_resident(w_fqkv, layer), _resident(cs.shape), tab, tab, tab],
        out_specs=[row(D_MODEL), row(QK_WIDTH), row(QK_WIDTH), row(V_WIDTH)],
        out_shape=[jax.ShapeDtypeStruct((m_rows, D_MODEL), BF),
                   jax.ShapeDtypeStruct((m_rows, QK_WIDTH), BF),
                   jax.ShapeDtypeStruct((m_rows, QK_WIDTH), BF),
                   jax.ShapeDtypeStruct((m_rows, V_WIDTH), BF)],
        compiler_params=_params(1, 48),
        name="in_proj_sample",
    )(hn, w_fqkv, cs, *rope_tabs)


def _fourier_stage1_kernel(y_ref, ff_ref, a_ref):
    pq = _dot(ff_ref[...], y_ref[0])
    p, q = pq[:N_SIDE], pq[N_SIDE:]
    w = FOURIER_WIDTH
    for s in range(S1_COLS):
        c = s * 2 * w
        a_ref[0, :, c:c + w] = (p[:, c:c + w] + q[:, c + w:c + 2 * w]).astype(BF)
        a_ref[0, :, c + w:c + 2 * w] = (p[:, c + w:c + 2 * w] - q[:, c:c + w]).astype(BF)


def _fourier_stage2_kernel(a_ref, gc_ref, gs_ref, fr_ref):
    w = FOURIER_WIDTH
    for kk in range(S2_ROWS):
        a = a_ref[0, kk * N_SIDE:(kk + 1) * N_SIDE, :]
        xr = _dot(gc_ref[kk], a[:, :w]) + _dot(gs_ref[kk], a[:, w:])
        fr_ref[0, :, kk * w:(kk + 1) * w] = xr.astype(BF)


def _fourier_sample(y, ff, gc, gs):
    nb = y.shape[0] // DEC_SEQ
    row_len = N_SIDE * D_MODEL
    y3 = y.reshape(nb, N_SIDE, row_len)
    cols = S1_COLS * D_MODEL
    a = pl.pallas_call(
        _fourier_stage1_kernel,
        grid=(nb, row_len // cols),
        in_specs=[pl.BlockSpec((1, N_SIDE, cols), lambda b, j: (b, 0, j)),
                  pl.BlockSpec(ff.shape, lambda b, j: (0, 0))],
        out_specs=pl.BlockSpec((1, N_SIDE, cols), lambda b, j: (b, 0, j)),
        out_shape=jax.ShapeDtypeStruct((nb, N_SIDE, row_len), BF),
        compiler_params=_params(2, 32),
        name="fourier_stage1",
    )(y3, ff)
    a = a.reshape(nb, N_SIDE * N_SIDE, D_MODEL)
    fr = pl.pallas_call(
        _fourier_stage2_kernel,
        grid=(nb, N_SIDE // S2_ROWS),
        in_specs=[pl.BlockSpec((1, S2_ROWS * N_SIDE, D_MODEL), lambda b, j: (b, j, 0)),
                  pl.BlockSpec((S2_ROWS, N_SIDE, N_SIDE), lambda b, j: (j, 0, 0)),
                  pl.BlockSpec((S2_ROWS, N_SIDE, N_SIDE), lambda b, j: (j, 0, 0))],
        out_specs=pl.BlockSpec((1, N_SIDE, S2_ROWS * FOURIER_WIDTH), lambda b, j: (b, 0, j)),
        out_shape=jax.ShapeDtypeStruct((nb, N_SIDE, N_SIDE * FOURIER_WIDTH), BF),
        compiler_params=_params(2, 32),
        name="fourier_stage2",
    )(a, gc, gs)
    return fr.reshape(nb * DEC_SEQ, FOURIER_WIDTH)


def _lam(lam_ref, lam_init):
    lp = lam_ref[0]
    a = jnp.sum(lp[0:1] * lp[1:2], axis=(0, 1), keepdims=True)
    b = jnp.sum(lp[2:3] * lp[3:4], axis=(0, 1), keepdims=True)
    return jnp.exp(a) - jnp.exp(b) + lam_init


def _split_maps(q):
    lane = lax.broadcasted_iota(jnp.int32, q.shape, 1)
    zero = jnp.zeros_like(q)
    return jnp.where(lane < QK_DIM, q, zero), jnp.where(lane >= QK_DIM, q, zero)


def _attn_prompt_kernel(q_ref, k_ref, v_ref, lam_ref, g_ref, o_ref, s_scr0, s_scr1, *, lam_init):
    lam = _lam(lam_ref, lam_init)
    s_scr = (s_scr0, s_scr1)

    def scores(h):
        sl = slice(h * V_DIM, (h + 1) * V_DIM)
        k = k_ref[0, 0, :, sl].astype(BF)
        maxes = []
        for mp, qz in enumerate(_split_maps(q_ref[:, sl])):
            s = _dot_nt(k, qz)
            s_scr[h % 2][mp] = s
            run = s[0:8, :]
            for r in range(1, SEQ // 8):
                run = jnp.maximum(run, s[r * 8:(r + 1) * 8, :])
            maxes.append(jnp.broadcast_to(jnp.max(run, axis=0, keepdims=True), (8, SEQ)))
        return maxes

    def values(h, maxes):
        sl = slice(h * V_DIM, (h + 1) * V_DIM)
        vt = v_ref[0, 0, :, sl].T.astype(BF)
        acc, l = [], []
        for mp in range(2):
            run = jnp.zeros((8, SEQ), F32)
            strips = []
            for r in range(SEQ // 16):
                p_lo = jnp.exp2(s_scr[h % 2][mp, r * 16:r * 16 + 8, :] - maxes[mp])
                p_hi = jnp.exp2(s_scr[h % 2][mp, r * 16 + 8:r * 16 + 16, :] - maxes[mp])
                run = run + p_lo + p_hi
                strips.append(jnp.concatenate([p_lo, p_hi], axis=0).astype(BF))
            l.append(jnp.sum(run, axis=0, keepdims=True))
            acc.append(_dot(vt, jnp.concatenate(strips, axis=0)))
        ot = acc[0] - (lam * l[0] / l[1]) * acc[1]
        ot = ot * lax.rsqrt(jnp.mean(ot * ot, axis=0, keepdims=True) + EPS * l[0] * l[0])
        o_ref[:, sl] = (ot.T * g_ref[0] * (1.0 - lam_init)).astype(BF)

    maxes = scores(0)
    for h in range(N_HEADS):
        following = scores(h + 1) if h + 1 < N_HEADS else None
        values(h, maxes)
        maxes = following


def _attn_prompt(q, state_k, state_v, lam_params, subln_g, layer):
    m_rows = q.shape[0]
    lam_init = 0.8 - 0.6 * math.exp(-0.3 * layer)
    st = pl.BlockSpec((1, 1, SEQ, D_MODEL), lambda b: (b, layer, 0, 0))
    return pl.pallas_call(
        functools.partial(_attn_prompt_kernel, lam_init=lam_init),
        grid=(m_rows // SEQ,),
        in_specs=[pl.BlockSpec((SEQ, QK_WIDTH), lambda b: (b, 0)), st, st,
                  pl.BlockSpec((1, 4, QK_DIM), lambda b: (layer, 0, 0)),
                  pl.BlockSpec((1, 1, V_DIM), lambda b: (layer, 0, 0))],
        out_specs=pl.BlockSpec((SEQ, V_WIDTH), lambda b: (b, 0)),
        out_shape=jax.ShapeDtypeStruct((m_rows, V_WIDTH), BF),
        scratch_shapes=[pltpu.VMEM((2, SEQ, SEQ), F32), pltpu.VMEM((2, SEQ, SEQ), F32)],
        compiler_params=_params(1, 32),
        name="attn_prompt",
    )(q, state_k, state_v, lam_params, subln_g.reshape(DEPTH, 1, V_DIM))


def _attn_sample_kernel(q_ref, kn_ref, kc_ref, vn_ref, vc_ref, lam_ref, g_ref, o_ref,
                        qz_scr, vt_new, vt_old, s_new0, s_new1, s_old0, s_old1, m_scr0, m_scr1,
                        acc_scr0, acc_scr1, l_scr0, l_scr1, *, lam_init, steps_per_head):
    s_new, s_old, m_scr = (s_new0, s_new1), (s_old0, s_old1), (m_scr0, m_scr1)
    acc_scr, l_scr = (acc_scr0, acc_scr1), (l_scr0, l_scr1)
    g = pl.program_id(0)

    @pl.when(g == 0)
    def _():
        for ref in (s_new1, s_old1, m_scr1, acc_scr1):
            ref[...] = jnp.zeros_like(ref)
        l_scr1[...] = jnp.ones_like(l_scr1)

    @pl.when(jnp.maximum(g - 1, 0) % steps_per_head == 0)
    def _():
        for c in range(DEC_SEQ // ATT_CK):
            vt_new[c] = vn_ref[c * ATT_CK:(c + 1) * ATT_CK, :].astype(F32).T.astype(BF)
        vt_old[...] = vc_ref[0, 0].T.astype(BF)

    def step(cur):
        prv = 1 - cur
        l1 = jnp.sum(l_scr[prv][0], axis=0, keepdims=True)
        l2 = jnp.sum(l_scr[prv][1], axis=0, keepdims=True)
        ot = acc_scr[prv][0] - (_lam(lam_ref, lam_init) * l1 / l2) * acc_scr[prv][1]
        ot = ot * lax.rsqrt(jnp.mean(ot * ot, axis=0, keepdims=True) + EPS * l1 * l1)
        o_ref[...] = (ot.T * g_ref[0] * (1.0 - lam_init)).astype(BF)

        q1, q2 = _split_maps(q_ref[...])
        qz_scr[0] = q1
        qz_scr[1] = q2

        def stages(k_c, vt_c, s_dst, s_src, stats):
            mx, l, acc = stats
            new_mx, new_l, new_acc = [], [], []
            n_keys = k_c.shape[0]
            for mp in range(2):
                s = _dot_nt(k_c, qz_scr[mp])
                s_dst[mp] = s
                run = mx[mp]
                for r in range(n_keys // 8):
                    run = jnp.maximum(run, s[r * 8:(r + 1) * 8, :])
                new_mx.append(run)
            for mp in range(2):
                m8 = m_scr[prv][mp]
                run = l[mp]
                strips = []
                for r in range(n_keys // 16):
                    p_lo = jnp.exp2(s_src[mp, r * 16:r * 16 + 8, :] - m8)
                    p_hi = jnp.exp2(s_src[mp, r * 16 + 8:r * 16 + 16, :] - m8)
                    run = run + p_lo + p_hi
                    strips.append(jnp.concatenate([p_lo, p_hi], axis=0).astype(BF))
                new_l.append(run)
                new_acc.append(acc[mp] + _dot(vt_c, jnp.concatenate(strips, axis=0)))
            return tuple(new_mx), tuple(new_l), tuple(new_acc)

        neg = jnp.full((8, ATT_Q), -jnp.inf, F32)
        zero = jnp.zeros((8, ATT_Q), F32)
        zero_acc = jnp.zeros((V_DIM, ATT_Q), F32)
        stats = stages(kc_ref[0, 0].astype(BF), vt_old[...], s_old[cur], s_old[prv],
                       ((neg, neg), (zero, zero), (zero_acc, zero_acc)))
        for c in range(DEC_SEQ // ATT_CK):
            stats = stages(kn_ref[c * ATT_CK:(c + 1) * ATT_CK, :], vt_new[c], s_new[cur].at[c],
                           s_new[prv].at[c], stats)
        mx, l, acc = stats

        for mp in range(2):
            m_scr[cur][mp] = jnp.broadcast_to(jnp.max(mx[mp], axis=0, keepdims=True), (8, ATT_Q))
            acc_scr[cur][mp] = acc[mp]
            l_scr[cur][mp] = l[mp]

    pl.when(g % 2 == 0)(lambda: step(0))
    pl.when(g % 2 == 1)(lambda: step(1))


def _attn_sample(q, k, v, cache_k, cache_v, lam_params, subln_g, layer):
    m_rows = q.shape[0]
    nq = DEC_SEQ // ATT_Q
    n_blocks = (m_rows // DEC_SEQ) * N_HEADS * nq
    nch = DEC_SEQ // ATT_CK
    lam_init = 0.8 - 0.6 * math.exp(-0.3 * layer)

    def decode(g, lag):
        n = jnp.clip(g - lag, 0, n_blocks - 1)
        return n // (N_HEADS * nq), (n // nq) % N_HEADS, n % nq

    def blk(lag):
        def index(g):
            b, h, i = decode(g, lag)
            return b * nq + i, h
        return pl.BlockSpec((ATT_Q, V_DIM), index)

    def new(lag):
        return pl.BlockSpec((DEC_SEQ, V_DIM), lambda g: decode(g, lag)[:2])

    def old(lag):
        def index(g):
            b, h, _ = decode(g, lag)
            return b, layer, 0, h
        return pl.BlockSpec((1, 1, PAST_LEN, V_DIM), index)

    per_slot = [pltpu.VMEM((nch, 2, ATT_CK, ATT_Q), F32), pltpu.VMEM((2, PAST_LEN, ATT_Q), F32),
                pltpu.VMEM((2, 8, ATT_Q), F32), pltpu.VMEM((2, V_DIM, ATT_Q), F32),
                pltpu.VMEM((2, 8, ATT_Q), F32)]
    return pl.pallas_call(
        functools.partial(_attn_sample_kernel, lam_init=lam_init, steps_per_head=nq),
        grid=(n_blocks + 2,),
        in_specs=[blk(0), new(0), old(0), new(1), old(1),
                  pl.BlockSpec((1, 4, QK_DIM), lambda g: (layer, 0, 0)),
                  pl.BlockSpec((1, 1, V_DIM), lambda g: (layer, 0, 0))],
        out_specs=blk(2),
        out_shape=jax.ShapeDtypeStruct((m_rows, V_WIDTH), BF),
        scratch_shapes=[pltpu.VMEM((2, ATT_Q, V_DIM), BF), pltpu.VMEM((nch, V_DIM, ATT_CK), BF),
                        pltpu.VMEM((V_DIM, PAST_LEN), BF)]
                       + [shape for shape in per_slot for _ in range(2)],
        compiler_params=_params(1, 56),
        name="attn_sample",
    )(q, k, cache_k, v, cache_v, lam_params, subln_g.reshape(DEPTH, 1, V_DIM))


def _mix_out_kernel(hn_ref, fr_ref, o_ref, x_ref, wg_ref, wf_ref, wa_ref, wo_ref, mods_ref, g2_ref,
                    xo_ref, hno_ref):
    hn = hn_ref[...]
    a_four = _dot(fr_ref[...], wf_ref[...])
    mixed = jax.nn.sigmoid(_dot(hn, wg_ref[:, :D_MODEL])) * a_four
    a_attn = _dot(o_ref[...], wa_ref[...])
    mixed += jax.nn.sigmoid(_dot(hn, wg_ref[:, D_MODEL:])) * a_attn
    mixed = mixed.astype(BF)
    m = mods_ref[0]
    for rows in _row_parts():
        x = x_ref[rows, :] + m[2:3] * _dot(mixed[rows], wo_ref[...])
        xo_ref[rows, :] = x
        hno_ref[rows, :] = _norm_mod(x, g2_ref[...], m[4:5], m[3:4]).astype(BF)


def _mix_out(hn, fr, o, x, w_gate, w_four, w_attn, w_o, mods, norm2_g, seq_len, layer):
    m_rows = hn.shape[0]
    row = lambda w: pl.BlockSpec((TM, w), lambda i: (i, 0))
    return pl.pallas_call(
        _mix_out_kernel,
        grid=(m_rows // TM,),
        in_specs=[row(D_MODEL), row(FOURIER_WIDTH), row(V_WIDTH), row(D_MODEL),
                  _layer_resident(w_gate, layer), _layer_resident(w_four, layer),
                  _layer_resident(w_attn, layer), _layer_resident(w_o, layer),
                  _mods_spec(mods, layer, seq_len), _layer_resident(norm2_g, layer)],
        out_specs=[row(D_MODEL), row(D_MODEL)],
        out_shape=[jax.ShapeDtypeStruct((m_rows, D_MODEL), F32),
                   jax.ShapeDtypeStruct((m_rows, D_MODEL), BF)],
        compiler_params=_params(1, 56),
        name="mix_out",
    )(hn, fr, o, x, w_gate, w_four, w_attn, w_o, mods[0], norm2_g)


def _ffn_kernel(hp_ref, hn_ref, hx_ref, x_ref, wup_ref, cw_ref, cb_ref, wdn_ref, mods_ref, gn_ref,
                modsn_ref, *rest, seq_len, last):
    if last:
        y_ref, lhs_scr, u_scr, h_scr = rest
    else:
        xo_ref, hno_ref, lhs_scr, u_scr, h_scr = rest
    i = pl.program_id(0)
    if seq_len % TM == 0:
        tiles = seq_len // TM
        blank = jnp.zeros(hp_ref.shape, hp_ref.dtype)
        hp = jnp.where(i % tiles == 0, blank, hp_ref[...])
        hx = jnp.where(i % tiles == tiles - 1, blank, hx_ref[...])
        edge = lambda rows, shift: rows
    else:
        hp, hx = hp_ref[...], hx_ref[...]
        pos = (i * TM + lax.broadcasted_iota(jnp.int32, (TM, 1), 0)) % seq_len
        keep = {-1: pos != 0, 1: pos != seq_len - 1}
        edge = lambda rows, shift: jnp.where(keep[shift], rows, 0.0)
    lhs_scr[0:HALO] = hp
    lhs_scr[HALO:HALO + TM] = hn_ref[...]
    lhs_scr[HALO + TM:] = hx

    def conv(col, slot):
        u_scr[slot] = _dot(lhs_scr[...], wup_ref[:, col:col + FF_CHUNK])
        prev = edge(u_scr[slot, pl.ds(HALO - 1, TM), :], -1)
        cur = u_scr[slot, pl.ds(HALO, TM), :]
        nxt = edge(u_scr[slot, pl.ds(HALO + 1, TM), :], 1)
        w = cw_ref[:, col:col + FF_CHUNK]
        return prev * w[0:1] + cur * w[1:2] + nxt * w[2:3] + cb_ref[:, col:col + FF_CHUNK]

    for c in range(D_FF // FF_CHUNK):
        val = conv(c * FF_CHUNK, 0)
        gate = conv(D_FF + c * FF_CHUNK, 1)
        h_scr[:, c * FF_CHUNK:(c + 1) * FF_CHUNK] = (gate * jax.nn.sigmoid(gate) * val).astype(BF)

    m = mods_ref[0]
    for rows in _row_parts():
        x = x_ref[rows, :] + m[5:6] * _dot(h_scr[rows, :], wdn_ref[...])
        if last:
            y_ref[rows, :] = x * lax.rsqrt(jnp.mean(x * x, axis=-1, keepdims=True) + EPS) * gn_ref[...]
        else:
            xo_ref[rows, :] = x
            mn = modsn_ref[0]
            hno_ref[rows, :] = _norm_mod(x, gn_ref[...], mn[1:2], mn[0:1]).astype(BF)


def _ffn(hn, x, w_up, conv_w, conv_b, w_down, mods, norm1_g, final_g, seq_len, layer):
    m_rows = hn.shape[0]
    last = layer == DEPTH - 1
    g_next, g_spec = (final_g, _resident(final_g.shape)) if last else (norm1_g, _layer_resident(norm1_g, layer + 1))
    per = TM // HALO
    row = lambda w: pl.BlockSpec((TM, w), lambda i: (i, 0))
    prev = pl.BlockSpec((HALO, D_MODEL), lambda i: (jnp.maximum(i * per - 1, 0), 0))
    nxt = pl.BlockSpec((HALO, D_MODEL), lambda i: (jnp.minimum((i + 1) * per, m_rows // HALO - 1), 0))
    if last:
        out_specs = row(D_MODEL)
        out_shape = jax.ShapeDtypeStruct((m_rows, D_MODEL), F32)
    else:
        out_specs = [row(D_MODEL), row(D_MODEL)]
        out_shape = [jax.ShapeDtypeStruct((m_rows, D_MODEL), F32),
                     jax.ShapeDtypeStruct((m_rows, D_MODEL), BF)]
    return pl.pallas_call(
        functools.partial(_ffn_kernel, seq_len=seq_len, last=last),
        grid=(m_rows // TM,),
        in_specs=[prev, row(D_MODEL), nxt, row(D_MODEL),
                  _layer_resident(w_up, layer), _layer_resident(conv_w, layer),
                  _layer_resident(conv_b, layer), _layer_resident(w_down, layer),
                  _mods_spec(mods, layer, seq_len), g_spec,
                  _mods_spec(mods, min(layer + 1, DEPTH - 1), seq_len)],
        out_specs=out_specs,
        out_shape=out_shape,
        scratch_shapes=[pltpu.VMEM((TM + 2 * HALO, D_MODEL), BF),
                        pltpu.VMEM((2, TM + 2 * HALO, FF_CHUNK), F32),
                        pltpu.VMEM((TM, D_FF), BF)],
        compiler_params=_params(1, 56),
        name="ffn",
    )(hn, hn, hn, x, w_up, conv_w, conv_b, w_down, mods[0], g_next, mods[0])


def _cos_sin(num, den):
    ang = (2.0 * math.pi / den) * (num % den).astype(F32)
    return jnp.cos(ang), jnp.sin(ang)


def _dft_tables():
    n = jnp.arange(FOURIER_GROUP_DIM)
    c, s = _cos_sin(n[:, None] * n[None, :], FOURIER_GROUP_DIM)
    cs = (jnp.concatenate([c, -s], axis=1) * FOURIER_GROUP_DIM ** -0.5).astype(BF)
    t = jnp.arange(SEQ)
    c, s = _cos_sin(t[:, None] * t[None, :], SEQ)
    ct, st = (c * SEQ ** -0.5).astype(BF), (s * SEQ ** -0.5).astype(BF)
    r = jnp.arange(N_SIDE)
    c, s = _cos_sin(r[:, None] * r[None, :], N_SIDE)
    ff = (jnp.concatenate([c, s], axis=0) * N_SIDE ** -0.5).astype(BF)
    k = r[:, None, None] + N_SIDE * r[None, :, None]
    c, s = _cos_sin(k * r[None, None, :], N_SIDE * N_SIDE)
    gc, gs = (c * N_SIDE ** -0.5).astype(BF), (s * N_SIDE ** -0.5).astype(BF)
    return cs, ct, st, ff, gc, gs


def _rope_tables():
    half = QK_DIM // 2
    inv_freq = 1.0 / (ROPE_BASE ** (jnp.arange(0, half, 2, dtype=F32) / half))
    t = jnp.arange(DEC_SEQ)
    row = (t // GRID_W).astype(F32)
    col = (t % GRID_W).astype(F32)
    lane = jnp.arange(128)
    d = lane % QK_DIM
    freq = inv_freq[d % (half // 2)]
    ang = jnp.where((d < half)[None, :], row[:, None] * freq[None, :], col[:, None] * freq[None, :])
    cos, sin = jnp.cos(ang), jnp.sin(ang)
    upper = ((d % half) >= half // 2)[None, :]
    sin_hi = jnp.where(upper, sin, 0.0)
    sin_lo = jnp.where(upper, 0.0, -sin)
    return cos, sin_lo, sin_hi


def _trunk(x, mods, w, seq_len, layer_fns):
    in_proj, fourier, attention = layer_fns
    hn = _prenorm(x, w["norm1_g"], mods, seq_len)
    for l in range(DEPTH):
        y, q, kv = in_proj(hn, l)
        fr = fourier(y)
        o = attention(q, kv, l)
        x, hn2 = _mix_out(hn, fr, o, x, w["w_gate"], w["w_fourier"], w["w_attn"], w["w_o"],
                          mods, w["norm2_g"], seq_len, l)
        res = _ffn(hn2, x, w["w_up"], w["conv_w"], w["conv_b"], w["w_down"],
                   mods, w["norm1_g"], w["final_g"], seq_len, l)
        if l == DEPTH - 1:
            return res
        x, hn = res


def kernel(x_prompt, x_sample, c, cache_k, cache_v, c_ctx, norm1_g, norm2_g, final_g, w_ada, b_ada, w_in,
           w_fourier, lam_params, subln_g, w_attn, w_o, w_up, conv_w, conv_b, w_down):
    w = {
        "norm1_g": norm1_g.reshape(DEPTH, 1, D_MODEL), "norm2_g": norm2_g.reshape(DEPTH, 1, D_MODEL),
        "final_g": final_g.reshape(1, D_MODEL),
        "w_fqkv": w_in[:, :, :FQKV_WIDTH].astype(BF), "w_gate": w_in[:, :, FQKV_WIDTH:].astype(BF),
        "w_fourier": w_fourier.astype(BF), "w_attn": w_attn.astype(BF), "w_o": w_o.astype(BF),
        "w_up": w_up.astype(BF), "w_down": w_down.astype(BF),
        "conv_w": conv_w, "conv_b": conv_b.reshape(DEPTH, 1, 2 * D_FF),
    }
    cond = jnp.concatenate([c_ctx[None, :], c, jnp.zeros((8 - 1 - DEC_BATCH, D_MODEL), F32)], axis=0)
    mods = _adaln(cond, w_ada, b_ada).reshape(DEPTH, cond.shape[0], N_MOD, D_MODEL)
    cs, ct, st, ff, gc, gs = _dft_tables()
    rope_tabs = _rope_tables()

    state = {"k": jnp.zeros((BATCH, DEPTH, SEQ, QK_WIDTH), F32),
             "v": jnp.zeros((BATCH, DEPTH, SEQ, V_WIDTH), F32)}

    def in_proj_p(hn, l):
        y, q, state["k"], state["v"] = _in_proj_prompt(hn, w["w_fqkv"], cs, ct, st, state["k"], state["v"], l)
        return y, q, None

    y_prompt = _trunk(
        x_prompt.reshape(BATCH * SEQ, D_MODEL), (mods, 0, 1), w, SEQ,
        (in_proj_p,
         lambda fr: fr,
         lambda q, kv, l: _attn_prompt(q, state["k"], state["v"], lam_params, subln_g, l)))

    ck = cache_k.reshape(DEC_BATCH, DEPTH, PAST_LEN, QK_WIDTH)
    cv = cache_v.reshape(DEC_BATCH, DEPTH, PAST_LEN, V_WIDTH)

    def in_proj_s(hn, l):
        y, q, k, v = _in_proj_sample(hn, w["w_fqkv"], cs, rope_tabs, l)
        return y, q, (k, v)

    y_sample = _trunk(
        x_sample.reshape(DEC_BATCH * DEC_SEQ, D_MODEL), (mods, 1, DEC_BATCH), w, DEC_SEQ,
        (in_proj_s,
         lambda y: _fourier_sample(y, ff, gc, gs),
         lambda q, kv, l: _attn_sample(q, kv[0], kv[1], ck, cv, lam_params, subln_g, l)))

    return (y_prompt.reshape(BATCH, SEQ, D_MODEL),
            y_sample.reshape(DEC_BATCH, DEC_SEQ, D_MODEL),
            state["k"].reshape(BATCH, DEPTH, SEQ, N_HEADS, 2, QK_DIM),
            state["v"].reshape(BATCH, DEPTH, SEQ, N_HEADS, V_DIM))
```

```python
import functools
import math

import jax
import jax.numpy as jnp
from jax import lax
from jax.experimental import pallas as pl
from jax.experimental.pallas import tpu as pltpu

D_MODEL = 1024
BATCH = 32
SEQ = 256
DEPTH = 4
DEC_BATCH = 2
DEC_SEQ = 4096
PAST_LEN = 256
GRID_W = 64
QK_DIM = 64
V_DIM = 2 * QK_DIM
N_HEADS = D_MODEL // (2 * QK_DIM)
FOURIER_GROUPS = 4
FOURIER_WIDTH = D_MODEL // 2
FOURIER_GROUP_DIM = FOURIER_WIDTH // FOURIER_GROUPS
QK_WIDTH = N_HEADS * 2 * QK_DIM
V_WIDTH = N_HEADS * V_DIM
GATE_WIDTH = 2 * D_MODEL
FQKV_WIDTH = FOURIER_WIDTH + 2 * QK_WIDTH + V_WIDTH
D_FF = ((8 * D_MODEL // 3 + 127) // 128) * 128
ROPE_BASE = 10000.0
EPS = 1e-6
N_MOD = 6
SCALE = QK_DIM ** -0.5
LOG2E = math.log2(math.e)

BF = jnp.bfloat16
F32 = jnp.float32

MIB = 1024 * 1024
TM = 512
TAIL_PARTS = 2
HALO = 16
FF_CHUNK = 256
AP_SEQS = 2
ATT_Q = 256
ATT_CK = 256
N_SIDE = 64
S1_COLS = 8
S2_ROWS = 16


def _params(n_axes, vmem_mib, flags=None):
    return pltpu.CompilerParams(dimension_semantics=("arbitrary",) * n_axes,
                                vmem_limit_bytes=vmem_mib * MIB, flags=flags)


def _resident(shape):
    zeros = (0,) * len(shape)
    return pl.BlockSpec(shape, lambda *_: zeros, pipeline_mode=pl.Buffered(1))


def _layer_resident(stacked, layer):
    tail = stacked.shape[1:]
    zeros = (0,) * len(tail)
    return pl.BlockSpec((None,) + tail, lambda *_: (layer,) + zeros, pipeline_mode=pl.Buffered(1))


def _dot(a, b):
    return jnp.dot(a, b, preferred_element_type=F32)


def _dot_nt(a, b):
    return lax.dot_general(a, b, (((1,), (1,)), ((), ())), preferred_element_type=F32)


def _row_parts():
    return [slice(p * TM // TAIL_PARTS, (p + 1) * TM // TAIL_PARTS) for p in range(TAIL_PARTS)]


def _norm_mod(x, g, sc, sh):
    y = x * lax.rsqrt(jnp.mean(x * x, axis=-1, keepdims=True) + EPS)
    return (y * g) * (1.0 + sc) + sh


def _adaln_kernel(c_ref, w_ref, b_ref, o_ref):
    c = c_ref[...]
    s = (c * jax.nn.sigmoid(c)).astype(BF)
    o_ref[0] = _dot(s, w_ref[0].astype(BF)) + b_ref[0]


def _adaln(cond, w_ada, b_ada):
    rows = cond.shape[0]
    return pl.pallas_call(
        _adaln_kernel,
        grid=(DEPTH, N_MOD),
        in_specs=[pl.BlockSpec((rows, D_MODEL), lambda l, j: (0, 0)),
                  pl.BlockSpec((1, D_MODEL, D_MODEL), lambda l, j: (l, 0, j)),
                  pl.BlockSpec((1, 1, D_MODEL), lambda l, j: (l, 0, j))],
        out_specs=pl.BlockSpec((1, rows, D_MODEL), lambda l, j: (l, 0, j)),
        out_shape=jax.ShapeDtypeStruct((DEPTH, rows, N_MOD * D_MODEL), F32),
        compiler_params=_params(2, 32),
        name="adaln",
    )(cond, w_ada, b_ada.reshape(DEPTH, 1, N_MOD * D_MODEL))


def _prenorm_kernel(x_ref, g_ref, mods_ref, hn_ref):
    m = mods_ref[0]
    hn_ref[...] = _norm_mod(x_ref[...], g_ref[...], m[1:2], m[0:1]).astype(BF)


def _mods_spec(mods, layer, seq_len):
    _, first, n_rows = mods
    if n_rows == 1:
        index = lambda i: (layer, first, 0, 0)
    else:
        index = lambda i: (layer, first + (i * TM) // seq_len, 0, 0)
    return pl.BlockSpec((None, 1, N_MOD, D_MODEL), index)


def _prenorm(x, g, mods, seq_len):
    m_rows = x.shape[0]
    row = pl.BlockSpec((TM, D_MODEL), lambda i: (i, 0))
    return pl.pallas_call(
        _prenorm_kernel,
        grid=(m_rows // TM,),
        in_specs=[row, _layer_resident(g, 0), _mods_spec(mods, 0, seq_len)],
        out_specs=row,
        out_shape=jax.ShapeDtypeStruct((m_rows, D_MODEL), BF),
        compiler_params=_params(1, 32),
        name="prenorm",
    )(x, g, mods[0])


def _channel_dft(f, cs_ref, y_ref):
    gd = FOURIER_GROUP_DIM
    for g in range(FOURIER_GROUPS):
        yg = _dot(f[:, g * gd:(g + 1) * gd], cs_ref[...])
        y_ref[:, g * gd:(g + 1) * gd] = yg[:, :gd].astype(BF)
        y_ref[:, FOURIER_WIDTH + g * gd:FOURIER_WIDTH + (g + 1) * gd] = yg[:, gd:].astype(BF)


def _rope(x, cos, sin_lo, sin_hi):
    cols = []
    for j in range(x.shape[1] // 128):
        xb = x[:, j * 128:(j + 1) * 128]
        cols.append(xb * cos + pltpu.roll(xb, 16, 1) * sin_hi + pltpu.roll(xb, 112, 1) * sin_lo)
    return jnp.concatenate(cols, axis=1)


def _in_proj_prompt_kernel(hn_ref, w_ref, cs_ref, ct_ref, st_ref, sk_any, sv_any, fr_ref, q_ref, sk_ref, sv_ref,
                           y_scr):
    del sk_any, sv_any
    hn = hn_ref[...]
    o = FOURIER_WIDTH
    f = _dot(hn, w_ref[:, 0:o]).astype(BF)
    _channel_dft(f, cs_ref, y_scr)
    for b in range(TM // SEQ):
        rows = slice(b * SEQ, (b + 1) * SEQ)
        fr = _dot(ct_ref[...], y_scr[rows, :o]) + _dot(st_ref[...], y_scr[rows, o:])
        fr_ref[rows, :] = fr.astype(BF)
    q_ref[...] = (_dot(hn, w_ref[:, o:o + QK_WIDTH]) * (SCALE * LOG2E)).astype(BF)
    o += QK_WIDTH
    sk_ref[...] = _dot(hn, w_ref[:, o:o + QK_WIDTH]).reshape(sk_ref.shape)
    o += QK_WIDTH
    sv_ref[...] = _dot(hn, w_ref[:, o:o + V_WIDTH]).reshape(sv_ref.shape)


def _in_proj_prompt(hn, w_fqkv, cs, ct, st_tab, state_k, state_v, layer):
    m_rows = hn.shape[0]
    nb = TM // SEQ
    row = lambda w: pl.BlockSpec((TM, w), lambda i: (i, 0))
    st = pl.BlockSpec((nb, 1, SEQ, D_MODEL), lambda i: (i, layer, 0, 0))
    any_spec = pl.BlockSpec(memory_space=pl.ANY)
    return pl.pallas_call(
        _in_proj_prompt_kernel,
        grid=(m_rows // TM,),
        in_specs=[row(D_MODEL), _layer_resident(w_fqkv, layer), _resident(cs.shape), _resident(ct.shape),
                  _resident(st_tab.shape), any_spec, any_spec],
        out_specs=[row(FOURIER_WIDTH), row(QK_WIDTH), st, st],
        out_shape=[jax.ShapeDtypeStruct((m_rows, FOURIER_WIDTH), BF),
                   jax.ShapeDtypeStruct((m_rows, QK_WIDTH), BF),
                   jax.ShapeDtypeStruct(state_k.shape, F32),
                   jax.ShapeDtypeStruct(state_v.shape, F32)],
        input_output_aliases={5: 2, 6: 3},
        scratch_shapes=[pltpu.VMEM((TM, D_MODEL), BF)],
        compiler_params=_params(1, 48),
        name="in_proj_prompt",
    )(hn, w_fqkv, cs, ct, st_tab, state_k, state_v)


def _in_proj_sample_kernel(hn_ref, w_ref, cs_ref, cos_ref, slo_ref, shi_ref, y_ref, q_ref, k_ref, v_ref):
    hn = hn_ref[...]
    o = FOURIER_WIDTH
    f = _dot(hn, w_ref[:, 0:o]).astype(BF)
    _channel_dft(f, cs_ref, y_ref)
    cos, slo, shi = cos_ref[...], slo_ref[...], shi_ref[...]
    q = _rope(_dot(hn, w_ref[:, o:o + QK_WIDTH]), cos, slo, shi)
    q_ref[...] = (q * (SCALE * LOG2E)).astype(BF)
    o += QK_WIDTH
    k_ref[...] = _rope(_dot(hn, w_ref[:, o:o + QK_WIDTH]), cos, slo, shi).astype(BF)
    o += QK_WIDTH
    v_ref[...] = _dot(hn, w_ref[:, o:o + V_WIDTH]).astype(BF)


def _in_proj_sample(hn, w_fqkv, cs, rope_tabs, layer):
    m_rows = hn.shape[0]
    row = lambda w: pl.BlockSpec((TM, w), lambda i: (i, 0))
    tab = pl.BlockSpec((TM, 128), lambda i: (i % (DEC_SEQ // TM), 0))
    return pl.pallas_call(
        _in_proj_sample_kernel,
        grid=(m_rows // TM,),
        in_specs=[row(D_MODEL), _layer_resident(w_fqkv, layer), _resident(cs.shape), tab, tab, tab],
        out_specs=[row(D_MODEL), row(QK_WIDTH), row(QK_WIDTH), row(V_WIDTH)],
        out_shape=[jax.ShapeDtypeStruct((m_rows, D_MODEL), BF),
                   jax.ShapeDtypeStruct((m_rows, QK_WIDTH), BF),
                   jax.ShapeDtypeStruct((m_rows, QK_WIDTH), BF),
                   jax.ShapeDtypeStruct((m_rows, V_WIDTH), BF)],
        compiler_params=_params(1, 48),
        name="in_proj_sample",
    )(hn, w_fqkv, cs, *rope_tabs)


def _fourier_stage1_kernel(y_ref, ff_ref, a_ref):
    pq = _dot(ff_ref[...], y_ref[0])
    p, q = pq[:N_SIDE], pq[N_SIDE:]
    w = FOURIER_WIDTH
    for s in range(S1_COLS):
        c = s * 2 * w
        a_ref[0, :, c:c + w] = (p[:, c:c + w] + q[:, c + w:c + 2 * w]).astype(BF)
        a_ref[0, :, c + w:c + 2 * w] = (p[:, c + w:c + 2 * w] - q[:, c:c + w]).astype(BF)


def _fourier_stage2_kernel(a_ref, gc_ref, gs_ref, fr_ref):
    w = FOURIER_WIDTH
    for kk in range(S2_ROWS):
        a = a_ref[0, kk * N_SIDE:(kk + 1) * N_SIDE, :]
        xr = _dot(gc_ref[kk], a[:, :w]) + _dot(gs_ref[kk], a[:, w:])
        fr_ref[0, :, kk * w:(kk + 1) * w] = xr.astype(BF)


def _fourier_sample(y, ff, gc, gs):
    nb = y.shape[0] // DEC_SEQ
    row_len = N_SIDE * D_MODEL
    y3 = y.reshape(nb, N_SIDE, row_len)
    cols = S1_COLS * D_MODEL
    a = pl.pallas_call(
        _fourier_stage1_kernel,
        grid=(nb, row_len // cols),
        in_specs=[pl.BlockSpec((1, N_SIDE, cols), lambda b, j: (b, 0, j)),
                  pl.BlockSpec(ff.shape, lambda b, j: (0, 0))],
        out_specs=pl.BlockSpec((1, N_SIDE, cols), lambda b, j: (b, 0, j)),
        out_shape=jax.ShapeDtypeStruct((nb, N_SIDE, row_len), BF),
        compiler_params=_params(2, 32),
        name="fourier_stage1",
    )(y3, ff)
    a = a.reshape(nb, N_SIDE * N_SIDE, D_MODEL)
    fr = pl.pallas_call(
        _fourier_stage2_kernel,
        grid=(nb, N_SIDE // S2_ROWS),
        in_specs=[pl.BlockSpec((1, S2_ROWS * N_SIDE, D_MODEL), lambda b, j: (b, j, 0)),
                  pl.BlockSpec((S2_ROWS, N_SIDE, N_SIDE), lambda b, j: (j, 0, 0)),
                  pl.BlockSpec((S2_ROWS, N_SIDE, N_SIDE), lambda b, j: (j, 0, 0))],
        out_specs=pl.BlockSpec((1, N_SIDE, S2_ROWS * FOURIER_WIDTH), lambda b, j: (b, 0, j)),
        out_shape=jax.ShapeDtypeStruct((nb, N_SIDE, N_SIDE * FOURIER_WIDTH), BF),
        compiler_params=_params(2, 32),
        name="fourier_stage2",
    )(a, gc, gs)
    return fr.reshape(nb * DEC_SEQ, FOURIER_WIDTH)


def _lam(lam_ref, lam_init):
    lp = lam_ref[0]
    a = jnp.sum(lp[0:1] * lp[1:2], axis=(0, 1), keepdims=True)
    b = jnp.sum(lp[2:3] * lp[3:4], axis=(0, 1), keepdims=True)
    return jnp.exp(a) - jnp.exp(b) + lam_init


def _split_maps(q):
    lane = lax.broadcasted_iota(jnp.int32, q.shape, 1)
    zero = jnp.zeros_like(q)
    return jnp.where(lane < QK_DIM, q, zero), jnp.where(lane >= QK_DIM, q, zero)


def _attn_prompt_kernel(q_ref, k_ref, v_ref, lam_ref, g_ref, o_ref, s_scr0, s_scr1, *, lam_init):
    lam = _lam(lam_ref, lam_init)
    s_scr = (s_scr0, s_scr1)

    def scores(u):
        b, h = divmod(u, N_HEADS)
        rows = slice(b * SEQ, (b + 1) * SEQ)
        sl = slice(h * V_DIM, (h + 1) * V_DIM)
        k = k_ref[b, 0, :, sl].astype(BF)
        maxes = []
        for mp, qz in enumerate(_split_maps(q_ref[rows, sl])):
            s = _dot_nt(k, qz)
            s_scr[u % 2][mp] = s
            run = s[0:8, :]
            for r in range(1, SEQ // 8):
                run = jnp.maximum(run, s[r * 8:(r + 1) * 8, :])
            maxes.append(jnp.broadcast_to(jnp.max(run, axis=0, keepdims=True), (8, SEQ)))
        return maxes

    def values(u, maxes):
        b, h = divmod(u, N_HEADS)
        rows = slice(b * SEQ, (b + 1) * SEQ)
        sl = slice(h * V_DIM, (h + 1) * V_DIM)
        vt = v_ref[b, 0, :, sl].T.astype(BF)
        acc, l = [], []
        for mp in range(2):
            run = jnp.zeros((8, SEQ), F32)
            strips = []
            for r in range(SEQ // 16):
                p_lo = jnp.exp2(s_scr[u % 2][mp, r * 16:r * 16 + 8, :] - maxes[mp])
                p_hi = jnp.exp2(s_scr[u % 2][mp, r * 16 + 8:r * 16 + 16, :] - maxes[mp])
                run = run + p_lo + p_hi
                strips.append(jnp.concatenate([p_lo, p_hi], axis=0).astype(BF))
            l.append(jnp.sum(run, axis=0, keepdims=True))
            acc.append(_dot(vt, jnp.concatenate(strips, axis=0)))
        ot = acc[0] - (lam * l[0] / l[1]) * acc[1]
        ot = ot * lax.rsqrt(jnp.mean(ot * ot, axis=0, keepdims=True) + EPS * l[0] * l[0])
        o_ref[rows, sl] = (ot.T * g_ref[0] * (1.0 - lam_init)).astype(BF)

    units = AP_SEQS * N_HEADS
    maxes = scores(0)
    for u in range(units):
        following = scores(u + 1) if u + 1 < units else None
        values(u, maxes)
        maxes = following


def _attn_prompt(q, state_k, state_v, lam_params, subln_g, layer):
    m_rows = q.shape[0]
    lam_init = 0.8 - 0.6 * math.exp(-0.3 * layer)
    st = pl.BlockSpec((AP_SEQS, 1, SEQ, D_MODEL), lambda b: (b, layer, 0, 0))
    return pl.pallas_call(
        functools.partial(_attn_prompt_kernel, lam_init=lam_init),
        grid=(m_rows // (AP_SEQS * SEQ),),
        in_specs=[pl.BlockSpec((AP_SEQS * SEQ, QK_WIDTH), lambda b: (b, 0)), st, st,
                  pl.BlockSpec((1, 4, QK_DIM), lambda b: (layer, 0, 0)),
                  pl.BlockSpec((1, 1, V_DIM), lambda b: (layer, 0, 0))],
        out_specs=pl.BlockSpec((AP_SEQS * SEQ, V_WIDTH), lambda b: (b, 0)),
        out_shape=jax.ShapeDtypeStruct((m_rows, V_WIDTH), BF),
        scratch_shapes=[pltpu.VMEM((2, SEQ, SEQ), F32), pltpu.VMEM((2, SEQ, SEQ), F32)],
        compiler_params=_params(1, 32),
        name="attn_prompt",
    )(q, state_k, state_v, lam_params, subln_g.reshape(DEPTH, 1, V_DIM))


def _attn_sample_kernel(q_ref, kn_ref, kc_ref, vn_ref, vc_ref, lam_ref, g_ref, o_ref,
                        qz_scr, vt_new, vt_old, s_new0, s_new1, s_old0, s_old1, m_scr0, m_scr1,
                        acc_scr0, acc_scr1, l_scr0, l_scr1, *, lam_init, steps_per_head):
    s_new, s_old, m_scr = (s_new0, s_new1), (s_old0, s_old1), (m_scr0, m_scr1)
    acc_scr, l_scr = (acc_scr0, acc_scr1), (l_scr0, l_scr1)
    g = pl.program_id(0)

    @pl.when(g == 0)
    def _():
        for ref in (s_new1, s_old1, m_scr1, acc_scr1):
            ref[...] = jnp.zeros_like(ref)
        l_scr1[...] = jnp.ones_like(l_scr1)

    @pl.when(jnp.maximum(g - 1, 0) % steps_per_head == 0)
    def _():
        for c in range(DEC_SEQ // ATT_CK):
            vt_new[c] = vn_ref[c * ATT_CK:(c + 1) * ATT_CK, :].astype(F32).T.astype(BF)
        vt_old[...] = vc_ref[0, 0].T.astype(BF)

    def step(cur):
        prv = 1 - cur
        l1 = jnp.sum(l_scr[prv][0], axis=0, keepdims=True)
        l2 = jnp.sum(l_scr[prv][1], axis=0, keepdims=True)
        ot = acc_scr[prv][0] - (_lam(lam_ref, lam_init) * l1 / l2) * acc_scr[prv][1]
        ot = ot * lax.rsqrt(jnp.mean(ot * ot, axis=0, keepdims=True) + EPS * l1 * l1)
        o_ref[...] = (ot.T * g_ref[0] * (1.0 - lam_init)).astype(BF)

        q1, q2 = _split_maps(q_ref[...])
        qz_scr[0] = q1
        qz_scr[1] = q2

        def stages(k_c, vt_c, s_dst, s_src, stats):
            mx, l, acc = stats
            new_mx, new_l, new_acc = [], [], []
            n_keys = k_c.shape[0]
            for mp in range(2):
                s = _dot_nt(k_c, qz_scr[mp])
                s_dst[mp] = s
                run = mx[mp]
                for r in range(n_keys // 8):
                    run = jnp.maximum(run, s[r * 8:(r + 1) * 8, :])
                new_mx.append(run)
            for mp in range(2):
                m8 = m_scr[prv][mp]
                run = l[mp]
                strips = []
                for r in range(n_keys // 16):
                    p_lo = jnp.exp2(s_src[mp, r * 16:r * 16 + 8, :] - m8)
                    p_hi = jnp.exp2(s_src[mp, r * 16 + 8:r * 16 + 16, :] - m8)
                    run = run + p_lo + p_hi
                    strips.append(jnp.concatenate([p_lo, p_hi], axis=0).astype(BF))
                new_l.append(run)
                new_acc.append(acc[mp] + _dot(vt_c, jnp.concatenate(strips, axis=0)))
            return tuple(new_mx), tuple(new_l), tuple(new_acc)

        neg = jnp.full((8, ATT_Q), -jnp.inf, F32)
        zero = jnp.zeros((8, ATT_Q), F32)
        zero_acc = jnp.zeros((V_DIM, ATT_Q), F32)
        stats = stages(kc_ref[0, 0].astype(BF), vt_old[...], s_old[cur], s_old[prv],
                       ((neg, neg), (zero, zero), (zero_acc, zero_acc)))
        for c in range(DEC_SEQ // ATT_CK):
            stats = stages(kn_ref[c * ATT_CK:(c + 1) * ATT_CK, :], vt_new[c], s_new[cur].at[c],
                           s_new[prv].at[c], stats)
        mx, l, acc = stats

        for mp in range(2):
            m_scr[cur][mp] = jnp.broadcast_to(jnp.max(mx[mp], axis=0, keepdims=True), (8, ATT_Q))
            acc_scr[cur][mp] = acc[mp]
            l_scr[cur][mp] = l[mp]

    pl.when(g % 2 == 0)(lambda: step(0))
    pl.when(g % 2 == 1)(lambda: step(1))


def _attn_sample(q, k, v, cache_k, cache_v, lam_params, subln_g, layer):
    m_rows = q.shape[0]
    nq = DEC_SEQ // ATT_Q
    n_blocks = (m_rows // DEC_SEQ) * N_HEADS * nq
    nch = DEC_SEQ // ATT_CK
    lam_init = 0.8 - 0.6 * math.exp(-0.3 * layer)

    def decode(g, lag):
        n = jnp.clip(g - lag, 0, n_blocks - 1)
        return n // (N_HEADS * nq), (n // nq) % N_HEADS, n % nq

    def blk(lag):
        def index(g):
            b, h, i = decode(g, lag)
            return b * nq + i, h
        return pl.BlockSpec((ATT_Q, V_DIM), index)

    def new(lag):
        return pl.BlockSpec((DEC_SEQ, V_DIM), lambda g: decode(g, lag)[:2])

    def old(lag):
        def index(g):
            b, h, _ = decode(g, lag)
            return b, layer, 0, h
        return pl.BlockSpec((1, 1, PAST_LEN, V_DIM), index)

    per_slot = [pltpu.VMEM((nch, 2, ATT_CK, ATT_Q), F32), pltpu.VMEM((2, PAST_LEN, ATT_Q), F32),
                pltpu.VMEM((2, 8, ATT_Q), F32), pltpu.VMEM((2, V_DIM, ATT_Q), F32),
                pltpu.VMEM((2, 8, ATT_Q), F32)]
    return pl.pallas_call(
        functools.partial(_attn_sample_kernel, lam_init=lam_init, steps_per_head=nq),
        grid=(n_blocks + 2,),
        in_specs=[blk(0), new(0), old(0), new(1), old(1),
                  pl.BlockSpec((1, 4, QK_DIM), lambda g: (layer, 0, 0)),
                  pl.BlockSpec((1, 1, V_DIM), lambda g: (layer, 0, 0))],
        out_specs=blk(2),
        out_shape=jax.ShapeDtypeStruct((m_rows, V_WIDTH), BF),
        scratch_shapes=[pltpu.VMEM((2, ATT_Q, V_DIM), BF), pltpu.VMEM((nch, V_DIM, ATT_CK), BF),
                        pltpu.VMEM((V_DIM, PAST_LEN), BF)]
                       + [shape for shape in per_slot for _ in range(2)],
        compiler_params=_params(1, 56),
        name="attn_sample",
    )(q, k, cache_k, v, cache_v, lam_params, subln_g.reshape(DEPTH, 1, V_DIM))


def _mix_out_kernel(hn_ref, fr_ref, o_ref, x_ref, wg_ref, wf_ref, wa_ref, wo_ref, mods_ref, g2_ref,
                    xo_ref, hno_ref):
    hn = hn_ref[...]
    a_four = _dot(fr_ref[...], wf_ref[...])
    mixed = jax.nn.sigmoid(_dot(hn, wg_ref[:, :D_MODEL])) * a_four
    a_attn = _dot(o_ref[...], wa_ref[...])
    mixed += jax.nn.sigmoid(_dot(hn, wg_ref[:, D_MODEL:])) * a_attn
    m = mods_ref[0]
    x = x_ref[...] + m[2:3] * _dot(mixed.astype(BF), wo_ref[...])
    xo_ref[...] = x
    hno_ref[...] = _norm_mod(x, g2_ref[...], m[4:5], m[3:4]).astype(BF)


def _mix_out(hn, fr, o, x, w_gate, w_four, w_attn, w_o, mods, norm2_g, seq_len, layer):
    m_rows = hn.shape[0]
    row = lambda w: pl.BlockSpec((TM, w), lambda i: (i, 0))
    return pl.pallas_call(
        _mix_out_kernel,
        grid=(m_rows // TM,),
        in_specs=[row(D_MODEL), row(FOURIER_WIDTH), row(V_WIDTH), row(D_MODEL),
                  _layer_resident(w_gate, layer), _layer_resident(w_four, layer),
                  _layer_resident(w_attn, layer), _layer_resident(w_o, layer),
                  _mods_spec(mods, layer, seq_len), _layer_resident(norm2_g, layer)],
        out_specs=[row(D_MODEL), row(D_MODEL)],
        out_shape=[jax.ShapeDtypeStruct((m_rows, D_MODEL), F32),
                   jax.ShapeDtypeStruct((m_rows, D_MODEL), BF)],
        compiler_params=_params(1, 56),
        name="mix_out",
    )(hn, fr, o, x, w_gate, w_four, w_attn, w_o, mods[0], norm2_g)


def _ffn_kernel(hp_ref, hn_ref, hx_ref, x_ref, wup_ref, cw_ref, cb_ref, wdn_ref, mods_ref, gn_ref,
                modsn_ref, *rest, seq_len, last):
    if last:
        y_ref, lhs_scr, u_scr, h_scr = rest
    else:
        xo_ref, hno_ref, lhs_scr, u_scr, h_scr = rest
    i = pl.program_id(0)
    if seq_len % TM == 0:
        tiles = seq_len // TM
        blank = jnp.zeros(hp_ref.shape, hp_ref.dtype)
        hp = jnp.where(i % tiles == 0, blank, hp_ref[...])
        hx = jnp.where(i % tiles == tiles - 1, blank, hx_ref[...])
        edge = lambda rows, shift: rows
    else:
        hp, hx = hp_ref[...], hx_ref[...]
        pos = (i * TM + lax.broadcasted_iota(jnp.int32, (TM, 1), 0)) % seq_len
        keep = {-1: pos != 0, 1: pos != seq_len - 1}
        edge = lambda rows, shift: jnp.where(keep[shift], rows, 0.0)
    lhs_scr[0:HALO] = hp
    lhs_scr[HALO:HALO + TM] = hn_ref[...]
    lhs_scr[HALO + TM:] = hx

    def conv(col, slot):
        u_scr[slot] = _dot(lhs_scr[...], wup_ref[:, col:col + FF_CHUNK])
        prev = edge(u_scr[slot, pl.ds(HALO - 1, TM), :], -1)
        cur = u_scr[slot, pl.ds(HALO, TM), :]
        nxt = edge(u_scr[slot, pl.ds(HALO + 1, TM), :], 1)
        w = cw_ref[:, col:col + FF_CHUNK]
        return prev * w[0:1] + cur * w[1:2] + nxt * w[2:3] + cb_ref[:, col:col + FF_CHUNK]

    for c in range(D_FF // FF_CHUNK):
        val = conv(c * FF_CHUNK, 0)
        gate = conv(D_FF + c * FF_CHUNK, 1)
        h_scr[:, c * FF_CHUNK:(c + 1) * FF_CHUNK] = (gate * jax.nn.sigmoid(gate) * val).astype(BF)

    m = mods_ref[0]
    for rows in _row_parts():
        x = x_ref[rows, :] + m[5:6] * _dot(h_scr[rows, :], wdn_ref[...])
        if last:
            y_ref[rows, :] = x * lax.rsqrt(jnp.mean(x * x, axis=-1, keepdims=True) + EPS) * gn_ref[...]
        else:
            xo_ref[rows, :] = x
            mn = modsn_ref[0]
            hno_ref[rows, :] = _norm_mod(x, gn_ref[...], mn[1:2], mn[0:1]).astype(BF)


def _ffn(hn, x, w_up, conv_w, conv_b, w_down, mods, norm1_g, final_g, seq_len, layer):
    m_rows = hn.shape[0]
    last = layer == DEPTH - 1
    g_next, g_spec = (final_g, _resident(final_g.shape)) if last else (norm1_g, _layer_resident(norm1_g, layer + 1))
    per = TM // HALO
    row = lambda w: pl.BlockSpec((TM, w), lambda i: (i, 0))
    prev = pl.BlockSpec((HALO, D_MODEL), lambda i: (jnp.maximum(i * per - 1, 0), 0))
    nxt = pl.BlockSpec((HALO, D_MODEL), lambda i: (jnp.minimum((i + 1) * per, m_rows // HALO - 1), 0))
    if last:
        out_specs = row(D_MODEL)
        out_shape = jax.ShapeDtypeStruct((m_rows, D_MODEL), F32)
    else:
        out_specs = [row(D_MODEL), row(D_MODEL)]
        out_shape = [jax.ShapeDtypeStruct((m_rows, D_MODEL), F32),
                     jax.ShapeDtypeStruct((m_rows, D_MODEL), BF)]
    return pl.pallas_call(
        functools.partial(_ffn_kernel, seq_len=seq_len, last=last),
        grid=(m_rows // TM,),
        in_specs=[prev, row(D_MODEL), nxt, row(D_MODEL),
                  _layer_resident(w_up, layer), _layer_resident(conv_w, layer),
                  _layer_resident(conv_b, layer), _layer_resident(w_down, layer),
                  _mods_spec(mods, layer, seq_len), g_spec,
                  _mods_spec(mods, min(layer + 1, DEPTH - 1), seq_len)],
        out_specs=out_specs,
        out_shape=out_shape,
        scratch_shapes=[pltpu.VMEM((TM + 2 * HALO, D_MODEL), BF),
                        pltpu.VMEM((2, TM + 2 * HALO, FF_CHUNK), F32),
                        pltpu.VMEM((TM, D_FF), BF)],
        compiler_params=_params(1, 56),
        name="ffn",
    )(hn, hn, hn, x, w_up, conv_w, conv_b, w_down, mods[0], g_next, mods[0])


def _cos_sin(num, den):
    ang = (2.0 * math.pi / den) * (num % den).astype(F32)
    return jnp.cos(ang), jnp.sin(ang)


def _dft_tables():
    n = jnp.arange(FOURIER_GROUP_DIM)
    c, s = _cos_sin(n[:, None] * n[None, :], FOURIER_GROUP_DIM)
    cs = (jnp.concatenate([c, -s], axis=1) * FOURIER_GROUP_DIM ** -0.5).astype(BF)
    t = jnp.arange(SEQ)
    c, s = _cos_sin(t[:, None] * t[None, :], SEQ)
    ct, st = (c * SEQ ** -0.5).astype(BF), (s * SEQ ** -0.5).astype(BF)
    r = jnp.arange(N_SIDE)
    c, s = _cos_sin(r[:, None] * r[None, :], N_SIDE)
    ff = (jnp.concatenate([c, s], axis=0) * N_SIDE ** -0.5).astype(BF)
    k = r[:, None, None] + N_SIDE * r[None, :, None]
    c, s = _cos_sin(k * r[None, None, :], N_SIDE * N_SIDE)
    gc, gs = (c * N_SIDE ** -0.5).astype(BF), (s * N_SIDE ** -0.5).astype(BF)
    return cs, ct, st, ff, gc, gs


def _rope_tables():
    half = QK_DIM // 2
    inv_freq = 1.0 / (ROPE_BASE ** (jnp.arange(0, half, 2, dtype=F32) / half))
    t = jnp.arange(DEC_SEQ)
    row = (t // GRID_W).astype(F32)
    col = (t % GRID_W).astype(F32)
    lane = jnp.arange(128)
    d = lane % QK_DIM
    freq = inv_freq[d % (half // 2)]
    ang = jnp.where((d < half)[None, :], row[:, None] * freq[None, :], col[:, None] * freq[None, :])
    cos, sin = jnp.cos(ang), jnp.sin(ang)
    upper = ((d % half) >= half // 2)[None, :]
    sin_hi = jnp.where(upper, sin, 0.0)
    sin_lo = jnp.where(upper, 0.0, -sin)
    return cos, sin_lo, sin_hi


def _trunk(x, mods, w, seq_len, layer_fns):
    in_proj, fourier, attention = layer_fns
    hn = _prenorm(x, w["norm1_g"], mods, seq_len)
    for l in range(DEPTH):
        y, q, kv = in_proj(hn, l)
        fr = fourier(y)
        o = attention(q, kv, l)
        x, hn2 = _mix_out(hn, fr, o, x, w["w_gate"], w["w_fourier"], w["w_attn"], w["w_o"],
                          mods, w["norm2_g"], seq_len, l)
        res = _ffn(hn2, x, w["w_up"], w["conv_w"], w["conv_b"], w["w_down"],
                   mods, w["norm1_g"], w["final_g"], seq_len, l)
        if l == DEPTH - 1:
            return res
        x, hn = res


def kernel(x_prompt, x_sample, c, cache_k, cache_v, c_ctx, norm1_g, norm2_g, final_g, w_ada, b_ada, w_in,
           w_fourier, lam_params, subln_g, w_attn, w_o, w_up, conv_w, conv_b, w_down):
    w = {
        "norm1_g": norm1_g.reshape(DEPTH, 1, D_MODEL), "norm2_g": norm2_g.reshape(DEPTH, 1, D_MODEL),
        "final_g": final_g.reshape(1, D_MODEL),
        "w_fqkv": w_in[:, :, :FQKV_WIDTH].astype(BF), "w_gate": w_in[:, :, FQKV_WIDTH:].astype(BF),
        "w_fourier": w_fourier.astype(BF), "w_attn": w_attn.astype(BF), "w_o": w_o.astype(BF),
        "w_up": w_up.astype(BF), "w_down": w_down.astype(BF),
        "conv_w": conv_w, "conv_b": conv_b.reshape(DEPTH, 1, 2 * D_FF),
    }
    cond = jnp.concatenate([c_ctx[None, :], c, jnp.zeros((8 - 1 - DEC_BATCH, D_MODEL), F32)], axis=0)
    mods = _adaln(cond, w_ada, b_ada).reshape(DEPTH, cond.shape[0], N_MOD, D_MODEL)
    cs, ct, st, ff, gc, gs = _dft_tables()
    rope_tabs = _rope_tables()

    state = {"k": jnp.zeros((BATCH, DEPTH, SEQ, QK_WIDTH), F32),
             "v": jnp.zeros((BATCH, DEPTH, SEQ, V_WIDTH), F32)}

    def in_proj_p(hn, l):
        y, q, state["k"], state["v"] = _in_proj_prompt(hn, w["w_fqkv"], cs, ct, st, state["k"], state["v"], l)
        return y, q, None

    y_prompt = _trunk(
        x_prompt.reshape(BATCH * SEQ, D_MODEL), (mods, 0, 1), w, SEQ,
        (in_proj_p,
         lambda fr: fr,
         lambda q, kv, l: _attn_prompt(q, state["k"], state["v"], lam_params, subln_g, l)))

    ck = cache_k.reshape(DEC_BATCH, DEPTH, PAST_LEN, QK_WIDTH)
    cv = cache_v.reshape(DEC_BATCH, DEPTH, PAST_LEN, V_WIDTH)

    def in_proj_s(hn, l):
        y, q, k, v = _in_proj_sample(hn, w["w_fqkv"], cs, rope_tabs, l)
        return y, q, (k, v)

    y_sample = _trunk(
        x_sample.reshape(DEC_BATCH * DEC_SEQ, D_MODEL), (mods, 1, DEC_BATCH), w, DEC_SEQ,
        (in_proj_s,
         lambda y: _fourier_sample(y, ff, gc, gs),
         lambda q, kv, l: _attn_sample(q, kv[0], kv[1], ck, cv, lam_params, subln_g, l)))

    return (y_prompt.reshape(BATCH, SEQ, D_MODEL),
            y_sample.reshape(DEC_BATCH, DEC_SEQ, D_MODEL),
            state["k"].reshape(BATCH, DEPTH, SEQ, N_HEADS, 2, QK_DIM),
            state["v"].reshape(BATCH, DEPTH, SEQ, N_HEADS, V_DIM))
```

```python
import functools
import math

import jax
import jax.numpy as jnp
from jax import lax
from jax.experimental import pallas as pl
from jax.experimental.pallas import tpu as pltpu

D_MODEL = 1024
BATCH = 32
SEQ = 256
DEPTH = 4
DEC_BATCH = 2
DEC_SEQ = 4096
PAST_LEN = 256
GRID_W = 64
QK_DIM = 64
V_DIM = 2 * QK_DIM
N_HEADS = D_MODEL // (2 * QK_DIM)
FOURIER_GROUPS = 4
FOURIER_WIDTH = D_MODEL // 2
FOURIER_GROUP_DIM = FOURIER_WIDTH // FOURIER_GROUPS
QK_WIDTH = N_HEADS * 2 * QK_DIM
V_WIDTH = N_HEADS * V_DIM
GATE_WIDTH = 2 * D_MODEL
FQKV_WIDTH = FOURIER_WIDTH + 2 * QK_WIDTH + V_WIDTH
D_FF = ((8 * D_MODEL // 3 + 127) // 128) * 128
ROPE_BASE = 10000.0
EPS = 1e-6
N_MOD = 6
SCALE = QK_DIM ** -0.5
LOG2E = math.log2(math.e)

BF = jnp.bfloat16
F32 = jnp.float32

MIB = 1024 * 1024
TM = 512
TAIL_PARTS = 2
HALO = 16
FF_CHUNK = 256
AP_SEQS = 2
ATT_Q = 256
ATT_CK = 256
N_SIDE = 64
S1_COLS = 8
S2_ROWS = 16


def _params(n_axes, vmem_mib, flags=None):
    return pltpu.CompilerParams(dimension_semantics=("arbitrary",) * n_axes,
                                vmem_limit_bytes=vmem_mib * MIB, flags=flags)


def _resident(shape):
    zeros = (0,) * len(shape)
    return pl.BlockSpec(shape, lambda *_: zeros, pipeline_mode=pl.Buffered(1))


def _layer_resident(stacked, layer):
    tail = stacked.shape[1:]
    zeros = (0,) * len(tail)
    return pl.BlockSpec((None,) + tail, lambda *_: (layer,) + zeros, pipeline_mode=pl.Buffered(1))


def _dot(a, b):
    return jnp.dot(a, b, preferred_element_type=F32)


def _dot_nt(a, b):
    return lax.dot_general(a, b, (((1,), (1,)), ((), ())), preferred_element_type=F32)


def _row_parts():
    return [slice(p * TM // TAIL_PARTS, (p + 1) * TM // TAIL_PARTS) for p in range(TAIL_PARTS)]


def _norm_mod(x, g, sc, sh):
    y = x * lax.rsqrt(jnp.mean(x * x, axis=-1, keepdims=True) + EPS)
    return (y * g) * (1.0 + sc) + sh


def _adaln_kernel(c_ref, w_ref, b_ref, o_ref):
    c = c_ref[...]
    s = (c * jax.nn.sigmoid(c)).astype(BF)
    o_ref[0] = _dot(s, w_ref[0].astype(BF)) + b_ref[0]


def _adaln(cond, w_ada, b_ada):
    rows = cond.shape[0]
    return pl.pallas_call(
        _adaln_kernel,
        grid=(DEPTH, N_MOD),
        in_specs=[pl.BlockSpec((rows, D_MODEL), lambda l, j: (0, 0)),
                  pl.BlockSpec((1, D_MODEL, D_MODEL), lambda l, j: (l, 0, j)),
                  pl.BlockSpec((1, 1, D_MODEL), lambda l, j: (l, 0, j))],
        out_specs=pl.BlockSpec((1, rows, D_MODEL), lambda l, j: (l, 0, j)),
        out_shape=jax.ShapeDtypeStruct((DEPTH, rows, N_MOD * D_MODEL), F32),
        compiler_params=_params(2, 32),
        name="adaln",
    )(cond, w_ada, b_ada.reshape(DEPTH, 1, N_MOD * D_MODEL))


def _prenorm_kernel(x_ref, g_ref, mods_ref, hn_ref):
    m = mods_ref[0]
    hn_ref[...] = _norm_mod(x_ref[...], g_ref[...], m[1:2], m[0:1]).astype(BF)


def _mods_spec(mods, layer, seq_len):
    _, first, n_rows = mods
    if n_rows == 1:
        index = lambda i: (layer, first, 0, 0)
    else:
        index = lambda i: (layer, first + (i * TM) // seq_len, 0, 0)
    return pl.BlockSpec((None, 1, N_MOD, D_MODEL), index)


def _prenorm(x, g, mods, seq_len):
    m_rows = x.shape[0]
    row = pl.BlockSpec((TM, D_MODEL), lambda i: (i, 0))
    return pl.pallas_call(
        _prenorm_kernel,
        grid=(m_rows // TM,),
        in_specs=[row, _layer_resident(g, 0), _mods_spec(mods, 0, seq_len)],
        out_specs=row,
        out_shape=jax.ShapeDtypeStruct((m_rows, D_MODEL), BF),
        compiler_params=_params(1, 32),
        name="prenorm",
    )(x, g, mods[0])


def _channel_dft(f, cs_ref, y_ref):
    gd = FOURIER_GROUP_DIM
    for g in range(FOURIER_GROUPS):
        yg = _dot(f[:, g * gd:(g + 1) * gd], cs_ref[...])
        y_ref[:, g * gd:(g + 1) * gd] = yg[:, :gd].astype(BF)
        y_ref[:, FOURIER_WIDTH + g * gd:FOURIER_WIDTH + (g + 1) * gd] = yg[:, gd:].astype(BF)


def _rope(x, cos, sin_lo, sin_hi):
    cols = []
    for j in range(x.shape[1] // 128):
        xb = x[:, j * 128:(j + 1) * 128]
        cols.append(xb * cos + pltpu.roll(xb, 16, 1) * sin_hi + pltpu.roll(xb, 112, 1) * sin_lo)
    return jnp.concatenate(cols, axis=1)


def _in_proj_prompt_kernel(hn_ref, w_ref, cs_ref, ct_ref, st_ref, sk_any, sv_any, fr_ref, q_ref, sk_ref, sv_ref,
                           kb_ref, vb_ref, y_scr):
    del sk_any, sv_any
    hn = hn_ref[...]
    o = FOURIER_WIDTH
    f = _dot(hn, w_ref[:, 0:o]).astype(BF)
    _channel_dft(f, cs_ref, y_scr)
    for b in range(TM // SEQ):
        rows = slice(b * SEQ, (b + 1) * SEQ)
        fr = _dot(ct_ref[...], y_scr[rows, :o]) + _dot(st_ref[...], y_scr[rows, o:])
        fr_ref[rows, :] = fr.astype(BF)
    q_ref[...] = (_dot(hn, w_ref[:, o:o + QK_WIDTH]) * (SCALE * LOG2E)).astype(BF)
    o += QK_WIDTH
    k = _dot(hn, w_ref[:, o:o + QK_WIDTH])
    sk_ref[...] = k.reshape(sk_ref.shape)
    kb_ref[...] = k.astype(BF)
    o += QK_WIDTH
    v = _dot(hn, w_ref[:, o:o + V_WIDTH])
    sv_ref[...] = v.reshape(sv_ref.shape)
    vb_ref[...] = v.astype(BF)


def _in_proj_prompt(hn, w_fqkv, cs, ct, st_tab, state_k, state_v, layer):
    m_rows = hn.shape[0]
    nb = TM // SEQ
    row = lambda w: pl.BlockSpec((TM, w), lambda i: (i, 0))
    st = pl.BlockSpec((nb, 1, SEQ, D_MODEL), lambda i: (i, layer, 0, 0))
    any_spec = pl.BlockSpec(memory_space=pl.ANY)
    return pl.pallas_call(
        _in_proj_prompt_kernel,
        grid=(m_rows // TM,),
        in_specs=[row(D_MODEL), _layer_resident(w_fqkv, layer), _resident(cs.shape), _resident(ct.shape),
                  _resident(st_tab.shape), any_spec, any_spec],
        out_specs=[row(FOURIER_WIDTH), row(QK_WIDTH), st, st, row(QK_WIDTH), row(V_WIDTH)],
        out_shape=[jax.ShapeDtypeStruct((m_rows, FOURIER_WIDTH), BF),
                   jax.ShapeDtypeStruct((m_rows, QK_WIDTH), BF),
                   jax.ShapeDtypeStruct(state_k.shape, F32),
                   jax.ShapeDtypeStruct(state_v.shape, F32),
                   jax.ShapeDtypeStruct((m_rows, QK_WIDTH), BF),
                   jax.ShapeDtypeStruct((m_rows, V_WIDTH), BF)],
        input_output_aliases={5: 2, 6: 3},
        scratch_shapes=[pltpu.VMEM((TM, D_MODEL), BF)],
        compiler_params=_params(1, 48),
        name="in_proj_prompt",
    )(hn, w_fqkv, cs, ct, st_tab, state_k, state_v)


def _in_proj_sample_kernel(hn_ref, w_ref, cs_ref, cos_ref, slo_ref, shi_ref, y_ref, q_ref, k_ref, v_ref):
    hn = hn_ref[...]
    o = FOURIER_WIDTH
    f = _dot(hn, w_ref[:, 0:o]).astype(BF)
    _channel_dft(f, cs_ref, y_ref)
    cos, slo, shi = cos_ref[...], slo_ref[...], shi_ref[...]
    q = _rope(_dot(hn, w_ref[:, o:o + QK_WIDTH]), cos, slo, shi)
    q_ref[...] = (q * (SCALE * LOG2E)).astype(BF)
    o += QK_WIDTH
    k_ref[...] = _rope(_dot(hn, w_ref[:, o:o + QK_WIDTH]), cos, slo, shi).astype(BF)
    o += QK_WIDTH
    v_ref[...] = _dot(hn, w_ref[:, o:o + V_WIDTH]).astype(BF)


def _in_proj_sample(hn, w_fqkv, cs, rope_tabs, layer):
    m_rows = hn.shape[0]
    row = lambda w: pl.BlockSpec((TM, w), lambda i: (i, 0))
    tab = pl.BlockSpec((TM, 128), lambda i: (i % (DEC_SEQ // TM), 0))
    return pl.pallas_call(
        _in_proj_sample_kernel,
        grid=(m_rows // TM,),
        in_specs=[row(D_MODEL), _layer_resident(w_fqkv, layer), _resident(cs.shape), tab, tab, tab],
        out_specs=[row(D_MODEL), row(QK_WIDTH), row(QK_WIDTH), row(V_WIDTH)],
        out_shape=[jax.ShapeDtypeStruct((m_rows, D_MODEL), BF),
                   jax.ShapeDtypeStruct((m_rows, QK_WIDTH), BF),
                   jax.ShapeDtypeStruct((m_rows, QK_WIDTH), BF),
                   jax.ShapeDtypeStruct((m_rows, V_WIDTH), BF)],
        compiler_params=_params(1, 48),
        name="in_proj_sample",
    )(hn, w_fqkv, cs, *rope_tabs)


def _fourier_stage1_kernel(y_ref, ff_ref, a_ref):
    pq = _dot(ff_ref[...], y_ref[0])
    p, q = pq[:N_SIDE], pq[N_SIDE:]
    w = FOURIER_WIDTH
    for s in range(S1_COLS):
        c = s * 2 * w
        a_ref[0, :, c:c + w] = (p[:, c:c + w] + q[:, c + w:c + 2 * w]).astype(BF)
        a_ref[0, :, c + w:c + 2 * w] = (p[:, c + w:c + 2 * w] - q[:, c:c + w]).astype(BF)


def _fourier_stage2_kernel(a_ref, gc_ref, gs_ref, fr_ref):
    w = FOURIER_WIDTH
    for kk in range(S2_ROWS):
        a = a_ref[0, kk * N_SIDE:(kk + 1) * N_SIDE, :]
        xr = _dot(gc_ref[kk], a[:, :w]) + _dot(gs_ref[kk], a[:, w:])
        fr_ref[0, :, kk * w:(kk + 1) * w] = xr.astype(BF)


def _fourier_sample(y, ff, gc, gs):
    nb = y.shape[0] // DEC_SEQ
    row_len = N_SIDE * D_MODEL
    y3 = y.reshape(nb, N_SIDE, row_len)
    cols = S1_COLS * D_MODEL
    a = pl.pallas_call(
        _fourier_stage1_kernel,
        grid=(nb, row_len // cols),
        in_specs=[pl.BlockSpec((1, N_SIDE, cols), lambda b, j: (b, 0, j)),
                  pl.BlockSpec(ff.shape, lambda b, j: (0, 0))],
        out_specs=pl.BlockSpec((1, N_SIDE, cols), lambda b, j: (b, 0, j)),
        out_shape=jax.ShapeDtypeStruct((nb, N_SIDE, row_len), BF),
        compiler_params=_params(2, 32),
        name="fourier_stage1",
    )(y3, ff)
    a = a.reshape(nb, N_SIDE * N_SIDE, D_MODEL)
    fr = pl.pallas_call(
        _fourier_stage2_kernel,
        grid=(nb, N_SIDE // S2_ROWS),
        in_specs=[pl.BlockSpec((1, S2_ROWS * N_SIDE, D_MODEL), lambda b, j: (b, j, 0)),
                  pl.BlockSpec((S2_ROWS, N_SIDE, N_SIDE), lambda b, j: (j, 0, 0)),
                  pl.BlockSpec((S2_ROWS, N_SIDE, N_SIDE), lambda b, j: (j, 0, 0))],
        out_specs=pl.BlockSpec((1, N_SIDE, S2_ROWS * FOURIER_WIDTH), lambda b, j: (b, 0, j)),
        out_shape=jax.ShapeDtypeStruct((nb, N_SIDE, N_SIDE * FOURIER_WIDTH), BF),
        compiler_params=_params(2, 32),
        name="fourier_stage2",
    )(a, gc, gs)
    return fr.reshape(nb * DEC_SEQ, FOURIER_WIDTH)


def _lam(lam_ref, lam_init):
    lp = lam_ref[0]
    a = jnp.sum(lp[0:1] * lp[1:2], axis=(0, 1), keepdims=True)
    b = jnp.sum(lp[2:3] * lp[3:4], axis=(0, 1), keepdims=True)
    return jnp.exp(a) - jnp.exp(b) + lam_init


def _split_maps(q):
    lane = lax.broadcasted_iota(jnp.int32, q.shape, 1)
    zero = jnp.zeros_like(q)
    return jnp.where(lane < QK_DIM, q, zero), jnp.where(lane >= QK_DIM, q, zero)


def _attn_prompt_kernel(q_ref, k_ref, v_ref, lam_ref, g_ref, o_ref, s_scr0, s_scr1, *, lam_init):
    lam = _lam(lam_ref, lam_init)
    s_scr = (s_scr0, s_scr1)

    def scores(u):
        b, h = divmod(u, N_HEADS)
        rows = slice(b * SEQ, (b + 1) * SEQ)
        sl = slice(h * V_DIM, (h + 1) * V_DIM)
        k = k_ref[rows, sl]
        maxes = []
        for mp, qz in enumerate(_split_maps(q_ref[rows, sl])):
            s = _dot_nt(k, qz)
            s_scr[u % 2][mp] = s
            run = s[0:8, :]
            for r in range(1, SEQ // 8):
                run = jnp.maximum(run, s[r * 8:(r + 1) * 8, :])
            maxes.append(jnp.broadcast_to(jnp.max(run, axis=0, keepdims=True), (8, SEQ)))
        return maxes

    def values(u, maxes):
        b, h = divmod(u, N_HEADS)
        rows = slice(b * SEQ, (b + 1) * SEQ)
        sl = slice(h * V_DIM, (h + 1) * V_DIM)
        vt = v_ref[rows, sl].astype(F32).T.astype(BF)
        acc, l = [], []
        for mp in range(2):
            run = jnp.zeros((8, SEQ), F32)
            strips = []
            for r in range(SEQ // 16):
                p_lo = jnp.exp2(s_scr[u % 2][mp, r * 16:r * 16 + 8, :] - maxes[mp])
                p_hi = jnp.exp2(s_scr[u % 2][mp, r * 16 + 8:r * 16 + 16, :] - maxes[mp])
                run = run + p_lo + p_hi
                strips.append(jnp.concatenate([p_lo, p_hi], axis=0).astype(BF))
            l.append(jnp.sum(run, axis=0, keepdims=True))
            acc.append(_dot(vt, jnp.concatenate(strips, axis=0)))
        ot = acc[0] - (lam * l[0] / l[1]) * acc[1]
        ot = ot * lax.rsqrt(jnp.mean(ot * ot, axis=0, keepdims=True) + EPS * l[0] * l[0])
        o_ref[rows, sl] = (ot.T * g_ref[0] * (1.0 - lam_init)).astype(BF)

    units = AP_SEQS * N_HEADS
    maxes = scores(0)
    for u in range(units):
        following = scores(u + 1) if u + 1 < units else None
        values(u, maxes)
        maxes = following


def _attn_prompt(q, k, v, lam_params, subln_g, layer):
    m_rows = q.shape[0]
    lam_init = 0.8 - 0.6 * math.exp(-0.3 * layer)
    st = pl.BlockSpec((AP_SEQS * SEQ, D_MODEL), lambda b: (b, 0))
    return pl.pallas_call(
        functools.partial(_attn_prompt_kernel, lam_init=lam_init),
        grid=(m_rows // (AP_SEQS * SEQ),),
        in_specs=[pl.BlockSpec((AP_SEQS * SEQ, QK_WIDTH), lambda b: (b, 0)), st, st,
                  pl.BlockSpec((1, 4, QK_DIM), lambda b: (layer, 0, 0)),
                  pl.BlockSpec((1, 1, V_DIM), lambda b: (layer, 0, 0))],
        out_specs=pl.BlockSpec((AP_SEQS * SEQ, V_WIDTH), lambda b: (b, 0)),
        out_shape=jax.ShapeDtypeStruct((m_rows, V_WIDTH), BF),
        scratch_shapes=[pltpu.VMEM((2, SEQ, SEQ), F32), pltpu.VMEM((2, SEQ, SEQ), F32)],
        compiler_params=_params(1, 32),
        name="attn_prompt",
    )(q, k, v, lam_params, subln_g.reshape(DEPTH, 1, V_DIM))


def _attn_sample_kernel(q_ref, kn_ref, kc_ref, vn_ref, vc_ref, lam_ref, g_ref, o_ref,
                        qz_scr, vt_new, vt_old, s_new0, s_new1, s_old0, s_old1, m_scr0, m_scr1,
                        acc_scr0, acc_scr1, l_scr0, l_scr1, *, lam_init, steps_per_head):
    s_new, s_old, m_scr = (s_new0, s_new1), (s_old0, s_old1), (m_scr0, m_scr1)
    acc_scr, l_scr = (acc_scr0, acc_scr1), (l_scr0, l_scr1)
    g = pl.program_id(0)

    @pl.when(g == 0)
    def _():
        for ref in (s_new1, s_old1, m_scr1, acc_scr1):
            ref[...] = jnp.zeros_like(ref)
        l_scr1[...] = jnp.ones_like(l_scr1)

    @pl.when(jnp.maximum(g - 1, 0) % steps_per_head == 0)
    def _():
        for c in range(DEC_SEQ // ATT_CK):
            vt_new[c] = vn_ref[c * ATT_CK:(c + 1) * ATT_CK, :].astype(F32).T.astype(BF)
        vt_old[...] = vc_ref[0, 0].T.astype(BF)

    def step(cur):
        prv = 1 - cur
        l1 = jnp.sum(l_scr[prv][0], axis=0, keepdims=True)
        l2 = jnp.sum(l_scr[prv][1], axis=0, keepdims=True)
        ot = acc_scr[prv][0] - (_lam(lam_ref, lam_init) * l1 / l2) * acc_scr[prv][1]
        ot = ot * lax.rsqrt(jnp.mean(ot * ot, axis=0, keepdims=True) + EPS * l1 * l1)
        o_ref[...] = (ot.T * g_ref[0] * (1.0 - lam_init)).astype(BF)

        q1, q2 = _split_maps(q_ref[...])
        qz_scr[0] = q1
        qz_scr[1] = q2

        def stages(k_c, vt_c, s_dst, s_src, stats):
            mx, l, acc = stats
            new_mx, new_l, new_acc = [], [], []
            n_keys = k_c.shape[0]
            for mp in range(2):
                s = _dot_nt(k_c, qz_scr[mp])
                s_dst[mp] = s
                run = mx[mp]
                for r in range(n_keys // 8):
                    run = jnp.maximum(run, s[r * 8:(r + 1) * 8, :])
                new_mx.append(run)
            for mp in range(2):
                m8 = m_scr[prv][mp]
                run = l[mp]
                strips = []
                for r in range(n_keys // 16):
                    p_lo = jnp.exp2(s_src[mp, r * 16:r * 16 + 8, :] - m8)
                    p_hi = jnp.exp2(s_src[mp, r * 16 + 8:r * 16 + 16, :] - m8)
                    run = run + p_lo + p_hi
                    strips.append(jnp.concatenate([p_lo, p_hi], axis=0).astype(BF))
                new_l.append(run)
                new_acc.append(acc[mp] + _dot(vt_c, jnp.concatenate(strips, axis=0)))
            return tuple(new_mx), tuple(new_l), tuple(new_acc)

        neg = jnp.full((8, ATT_Q), -jnp.inf, F32)
        zero = jnp.zeros((8, ATT_Q), F32)
        zero_acc = jnp.zeros((V_DIM, ATT_Q), F32)
        stats = stages(kc_ref[0, 0].astype(BF), vt_old[...], s_old[cur], s_old[prv],
                       ((neg, neg), (zero, zero), (zero_acc, zero_acc)))
        for c in range(DEC_SEQ // ATT_CK):
            stats = stages(kn_ref[c * ATT_CK:(c + 1) * ATT_CK, :], vt_new[c], s_new[cur].at[c],
                           s_new[prv].at[c], stats)
        mx, l, acc = stats

        for mp in range(2):
            m_scr[cur][mp] = jnp.broadcast_to(jnp.max(mx[mp], axis=0, keepdims=True), (8, ATT_Q))
            acc_scr[cur][mp] = acc[mp]
            l_scr[cur][mp] = l[mp]

    pl.when(g % 2 == 0)(lambda: step(0))
    pl.when(g % 2 == 1)(lambda: step(1))


def _attn_sample(q, k, v, cache_k, cache_v, lam_params, subln_g, layer):
    m_rows = q.shape[0]
    nq = DEC_SEQ // ATT_Q
    n_blocks = (m_rows // DEC_SEQ) * N_HEADS * nq
    nch = DEC_SEQ // ATT_CK
    lam_init = 0.8 - 0.6 * math.exp(-0.3 * layer)

    def decode(g, lag):
        n = jnp.clip(g - lag, 0, n_blocks - 1)
        return n // (N_HEADS * nq), (n // nq) % N_HEADS, n % nq

    def blk(lag):
        def index(g):
            b, h, i = decode(g, lag)
            return b * nq + i, h
        return pl.BlockSpec((ATT_Q, V_DIM), index)

    def new(lag):
        return pl.BlockSpec((DEC_SEQ, V_DIM), lambda g: decode(g, lag)[:2])

    def old(lag):
        def index(g):
            b, h, _ = decode(g, lag)
            return b, layer, 0, h
        return pl.BlockSpec((1, 1, PAST_LEN, V_DIM), index)

    per_slot = [pltpu.VMEM((nch, 2, ATT_CK, ATT_Q), F32), pltpu.VMEM((2, PAST_LEN, ATT_Q), F32),
                pltpu.VMEM((2, 8, ATT_Q), F32), pltpu.VMEM((2, V_DIM, ATT_Q), F32),
                pltpu.VMEM((2, 8, ATT_Q), F32)]
    return pl.pallas_call(
        functools.partial(_attn_sample_kernel, lam_init=lam_init, steps_per_head=nq),
        grid=(n_blocks + 2,),
        in_specs=[blk(0), new(0), old(0), new(1), old(1),
                  pl.BlockSpec((1, 4, QK_DIM), lambda g: (layer, 0, 0)),
                  pl.BlockSpec((1, 1, V_DIM), lambda g: (layer, 0, 0))],
        out_specs=blk(2),
        out_shape=jax.ShapeDtypeStruct((m_rows, V_WIDTH), BF),
        scratch_shapes=[pltpu.VMEM((2, ATT_Q, V_DIM), BF), pltpu.VMEM((nch, V_DIM, ATT_CK), BF),
                        pltpu.VMEM((V_DIM, PAST_LEN), BF)]
                       + [shape for shape in per_slot for _ in range(2)],
        compiler_params=_params(1, 56),
        name="attn_sample",
    )(q, k, cache_k, v, cache_v, lam_params, subln_g.reshape(DEPTH, 1, V_DIM))


def _mix_out_kernel(hn_ref, fr_ref, o_ref, x_ref, wg_ref, wf_ref, wa_ref, wo_ref, mods_ref, g2_ref,
                    xo_ref, hno_ref):
    hn = hn_ref[...]
    a_four = _dot(fr_ref[...], wf_ref[...])
    mixed = jax.nn.sigmoid(_dot(hn, wg_ref[:, :D_MODEL])) * a_four
    a_attn = _dot(o_ref[...], wa_ref[...])
    mixed += jax.nn.sigmoid(_dot(hn, wg_ref[:, D_MODEL:])) * a_attn
    m = mods_ref[0]
    x = x_ref[...] + m[2:3] * _dot(mixed.astype(BF), wo_ref[...])
    xo_ref[...] = x
    hno_ref[...] = _norm_mod(x, g2_ref[...], m[4:5], m[3:4]).astype(BF)


def _mix_out(hn, fr, o, x, w_gate, w_four, w_attn, w_o, mods, norm2_g, seq_len, layer):
    m_rows = hn.shape[0]
    row = lambda w: pl.BlockSpec((TM, w), lambda i: (i, 0))
    return pl.pallas_call(
        _mix_out_kernel,
        grid=(m_rows // TM,),
        in_specs=[row(D_MODEL), row(FOURIER_WIDTH), row(V_WIDTH), row(D_MODEL),
                  _layer_resident(w_gate, layer), _layer_resident(w_four, layer),
                  _layer_resident(w_attn, layer), _layer_resident(w_o, layer),
                  _mods_spec(mods, layer, seq_len), _layer_resident(norm2_g, layer)],
        out_specs=[row(D_MODEL), row(D_MODEL)],
        out_shape=[jax.ShapeDtypeStruct((m_rows, D_MODEL), F32),
                   jax.ShapeDtypeStruct((m_rows, D_MODEL), BF)],
        compiler_params=_params(1, 56),
        name="mix_out",
    )(hn, fr, o, x, w_gate, w_four, w_attn, w_o, mods[0], norm2_g)


def _ffn_kernel(hp_ref, hn_ref, hx_ref, x_ref, wup_ref, cw_ref, cb_ref, wdn_ref, mods_ref, gn_ref,
                modsn_ref, *rest, seq_len, last):
    if last:
        y_ref, lhs_scr, u_scr, h_scr = rest
    else:
        xo_ref, hno_ref, lhs_scr, u_scr, h_scr = rest
    i = pl.program_id(0)
    if seq_len % TM == 0:
        tiles = seq_len // TM
        blank = jnp.zeros(hp_ref.shape, hp_ref.dtype)
        hp = jnp.where(i % tiles == 0, blank, hp_ref[...])
        hx = jnp.where(i % tiles == tiles - 1, blank, hx_ref[...])
        edge = lambda rows, shift: rows
    else:
        hp, hx = hp_ref[...], hx_ref[...]
        pos = (i * TM + lax.broadcasted_iota(jnp.int32, (TM, 1), 0)) % seq_len
        keep = {-1: pos != 0, 1: pos != seq_len - 1}
        edge = lambda rows, shift: jnp.where(keep[shift], rows, 0.0)
    lhs_scr[0:HALO] = hp
    lhs_scr[HALO:HALO + TM] = hn_ref[...]
    lhs_scr[HALO + TM:] = hx

    def conv(col, slot):
        u_scr[slot] = _dot(lhs_scr[...], wup_ref[:, col:col + FF_CHUNK])
        prev = edge(u_scr[slot, pl.ds(HALO - 1, TM), :], -1)
        cur = u_scr[slot, pl.ds(HALO, TM), :]
        nxt = edge(u_scr[slot, pl.ds(HALO + 1, TM), :], 1)
        w = cw_ref[:, col:col + FF_CHUNK]
        return prev * w[0:1] + cur * w[1:2] + nxt * w[2:3] + cb_ref[:, col:col + FF_CHUNK]

    for c in range(D_FF // FF_CHUNK):
        val = conv(c * FF_CHUNK, 0)
        gate = conv(D_FF + c * FF_CHUNK, 1)
        h_scr[:, c * FF_CHUNK:(c + 1) * FF_CHUNK] = (gate * jax.nn.sigmoid(gate) * val).astype(BF)

    m = mods_ref[0]
    for rows in _row_parts():
        x = x_ref[rows, :] + m[5:6] * _dot(h_scr[rows, :], wdn_ref[...])
        if last:
            y_ref[rows, :] = x * lax.rsqrt(jnp.mean(x * x, axis=-1, keepdims=True) + EPS) * gn_ref[...]
        else:
            xo_ref[rows, :] = x
            mn = modsn_ref[0]
            hno_ref[rows, :] = _norm_mod(x, gn_ref[...], mn[1:2], mn[0:1]).astype(BF)


def _ffn(hn, x, w_up, conv_w, conv_b, w_down, mods, norm1_g, final_g, seq_len, layer):
    m_rows = hn.shape[0]
    last = layer == DEPTH - 1
    g_next, g_spec = (final_g, _resident(final_g.shape)) if last else (norm1_g, _layer_resident(norm1_g, layer + 1))
    per = TM // HALO
    row = lambda w: pl.BlockSpec((TM, w), lambda i: (i, 0))
    prev = pl.BlockSpec((HALO, D_MODEL), lambda i: (jnp.maximum(i * per - 1, 0), 0))
    nxt = pl.BlockSpec((HALO, D_MODEL), lambda i: (jnp.minimum((i + 1) * per, m_rows // HALO - 1), 0))
    if last:
        out_specs = row(D_MODEL)
        out_shape = jax.ShapeDtypeStruct((m_rows, D_MODEL), F32)
    else:
        out_specs = [row(D_MODEL), row(D_MODEL)]
        out_shape = [jax.ShapeDtypeStruct((m_rows, D_MODEL), F32),
                     jax.ShapeDtypeStruct((m_rows, D_MODEL), BF)]
    return pl.pallas_call(
        functools.partial(_ffn_kernel, seq_len=seq_len, last=last),
        grid=(m_rows // TM,),
        in_specs=[prev, row(D_MODEL), nxt, row(D_MODEL),
                  _layer_resident(w_up, layer), _layer_resident(conv_w, layer),
                  _layer_resident(conv_b, layer), _layer_resident(w_down, layer),
                  _mods_spec(mods, layer, seq_len), g_spec,
                  _mods_spec(mods, min(layer + 1, DEPTH - 1), seq_len)],
        out_specs=out_specs,
        out_shape=out_shape,
        scratch_shapes=[pltpu.VMEM((TM + 2 * HALO, D_MODEL), BF),
                        pltpu.VMEM((2, TM + 2 * HALO, FF_CHUNK), F32),
                        pltpu.VMEM((TM, D_FF), BF)],
        compiler_params=_params(1, 56),
        name="ffn",
    )(hn, hn, hn, x, w_up, conv_w, conv_b, w_down, mods[0], g_next, mods[0])


def _cos_sin(num, den):
    ang = (2.0 * math.pi / den) * (num % den).astype(F32)
    return jnp.cos(ang), jnp.sin(ang)


def _dft_tables():
    n = jnp.arange(FOURIER_GROUP_DIM)
    c, s = _cos_sin(n[:, None] * n[None, :], FOURIER_GROUP_DIM)
    cs = (jnp.concatenate([c, -s], axis=1) * FOURIER_GROUP_DIM ** -0.5).astype(BF)
    t = jnp.arange(SEQ)
    c, s = _cos_sin(t[:, None] * t[None, :], SEQ)
    ct, st = (c * SEQ ** -0.5).astype(BF), (s * SEQ ** -0.5).astype(BF)
    r = jnp.arange(N_SIDE)
    c, s = _cos_sin(r[:, None] * r[None, :], N_SIDE)
    ff = (jnp.concatenate([c, s], axis=0) * N_SIDE ** -0.5).astype(BF)
    k = r[:, None, None] + N_SIDE * r[None, :, None]
    c, s = _cos_sin(k * r[None, None, :], N_SIDE * N_SIDE)
    gc, gs = (c * N_SIDE ** -0.5).astype(BF), (s * N_SIDE ** -0.5).astype(BF)
    return cs, ct, st, ff, gc, gs


def _rope_tables():
    half = QK_DIM // 2
    inv_freq = 1.0 / (ROPE_BASE ** (jnp.arange(0, half, 2, dtype=F32) / half))
    t = jnp.arange(DEC_SEQ)
    row = (t // GRID_W).astype(F32)
    col = (t % GRID_W).astype(F32)
    lane = jnp.arange(128)
    d = lane % QK_DIM
    freq = inv_freq[d % (half // 2)]
    ang = jnp.where((d < half)[None, :], row[:, None] * freq[None, :], col[:, None] * freq[None, :])
    cos, sin = jnp.cos(ang), jnp.sin(ang)
    upper = ((d % half) >= half // 2)[None, :]
    sin_hi = jnp.where(upper, sin, 0.0)
    sin_lo = jnp.where(upper, 0.0, -sin)
    return cos, sin_lo, sin_hi


def _trunk(x, mods, w, seq_len, layer_fns):
    in_proj, fourier, attention = layer_fns
    hn = _prenorm(x, w["norm1_g"], mods, seq_len)
    for l in range(DEPTH):
        y, q, kv = in_proj(hn, l)
        fr = fourier(y)
        o = attention(q, kv, l)
        x, hn2 = _mix_out(hn, fr, o, x, w["w_gate"], w["w_fourier"], w["w_attn"], w["w_o"],
                          mods, w["norm2_g"], seq_len, l)
        res = _ffn(hn2, x, w["w_up"], w["conv_w"], w["conv_b"], w["w_down"],
                   mods, w["norm1_g"], w["final_g"], seq_len, l)
        if l == DEPTH - 1:
            return res
        x, hn = res


def kernel(x_prompt, x_sample, c, cache_k, cache_v, c_ctx, norm1_g, norm2_g, final_g, w_ada, b_ada, w_in,
           w_fourier, lam_params, subln_g, w_attn, w_o, w_up, conv_w, conv_b, w_down):
    w = {
        "norm1_g": norm1_g.reshape(DEPTH, 1, D_MODEL), "norm2_g": norm2_g.reshape(DEPTH, 1, D_MODEL),
        "final_g": final_g.reshape(1, D_MODEL),
        "w_fqkv": w_in[:, :, :FQKV_WIDTH].astype(BF), "w_gate": w_in[:, :, FQKV_WIDTH:].astype(BF),
        "w_fourier": w_fourier.astype(BF), "w_attn": w_attn.astype(BF), "w_o": w_o.astype(BF),
        "w_up": w_up.astype(BF), "w_down": w_down.astype(BF),
        "conv_w": conv_w, "conv_b": conv_b.reshape(DEPTH, 1, 2 * D_FF),
    }
    cond = jnp.concatenate([c_ctx[None, :], c, jnp.zeros((8 - 1 - DEC_BATCH, D_MODEL), F32)], axis=0)
    mods = _adaln(cond, w_ada, b_ada).reshape(DEPTH, cond.shape[0], N_MOD, D_MODEL)
    cs, ct, st, ff, gc, gs = _dft_tables()
    rope_tabs = _rope_tables()

    state = {"k": jnp.zeros((BATCH, DEPTH, SEQ, QK_WIDTH), F32),
             "v": jnp.zeros((BATCH, DEPTH, SEQ, V_WIDTH), F32)}

    def in_proj_p(hn, l):
        y, q, state["k"], state["v"], kb, vb = _in_proj_prompt(hn, w["w_fqkv"], cs, ct, st, state["k"],
                                                               state["v"], l)
        return y, q, (kb, vb)

    y_prompt = _trunk(
        x_prompt.reshape(BATCH * SEQ, D_MODEL), (mods, 0, 1), w, SEQ,
        (in_proj_p,
         lambda fr: fr,
         lambda q, kv, l: _attn_prompt(q, kv[0], kv[1], lam_params, subln_g, l)))

    ck = cache_k.reshape(DEC_BATCH, DEPTH, PAST_LEN, QK_WIDTH)
    cv = cache_v.reshape(DEC_BATCH, DEPTH, PAST_LEN, V_WIDTH)

    def in_proj_s(hn, l):
        y, q, k, v = _in_proj_sample(hn, w["w_fqkv"], cs, rope_tabs, l)
        return y, q, (k, v)

    y_sample = _trunk(
        x_sample.reshape(DEC_BATCH * DEC_SEQ, D_MODEL), (mods, 1, DEC_BATCH), w, DEC_SEQ,
        (in_proj_s,
         lambda y: _fourier_sample(y, ff, gc, gs),
         lambda q, kv, l: _attn_sample(q, kv[0], kv[1], ck, cv, lam_params, subln_g, l)))

    return (y_prompt.reshape(BATCH, SEQ, D_MODEL),
            y_sample.reshape(DEC_BATCH, DEC_SEQ, D_MODEL),
            state["k"].reshape(BATCH, DEPTH, SEQ, N_HEADS, 2, QK_DIM),
            state["v"].reshape(BATCH, DEPTH, SEQ, N_HEADS, V_DIM))
```
